```python
import jax, jax.numpy as jnp
from jax import lax
import numpy as np

D_MODEL = 1024
BATCH = 16
SEQ = 2048
DEPTH = 1
DEC_BATCH = 128
DEC_SEQ = 1
PAST_LEN = 16384
PAGE_SIZE = 128

N_HEADS = 8
N_KV_HEADS = 2
HEAD_DIM = 64
Q_GROUP = N_HEADS // N_KV_HEADS
WINDOW = 128
ATT_BLOCK = WINDOW
ATT_W = N_HEADS * HEAD_DIM
KV_W = N_KV_HEADS * HEAD_DIM
POOL_WINDOWS = (2, 4, 8, 16)
POOL_GROUPS = 4
POOL_W = D_MODEL // 2
POOL_GC = POOL_W // POOL_GROUPS
POOL_HIST = max(POOL_WINDOWS) - 1
IN_W = ATT_W + 2 * KV_W + POOL_W + 2 * D_MODEL
PEER_HEADS = 8
PEER_NKEYS = 128
PEER_EXPERTS = PEER_NKEYS * PEER_NKEYS
PEER_DQ = 256
PEER_TOPK = 16
PEER_BLOCK = 128
NORM_EPS = 1e-6

kernel_name = 'hybrid_swa_pool_peer_step'


def rms_norm(x, g):
    xf = x.astype(jnp.float32)
    y = xf * lax.rsqrt(jnp.mean(xf * xf, axis=-1, keepdims=True) + NORM_EPS)
    return (y * g.astype(jnp.float32)).astype(x.dtype)


def alibi_slopes():
    return jnp.exp2(-8.0 * jnp.arange(1, N_HEADS + 1, dtype=jnp.float32) / N_HEADS)


def adaln(c, w_ada, b_ada):
    m = jax.nn.silu(c) @ w_ada + b_ada
    return jnp.split(m[:, None, :], 6, axis=-1)


def sink_attention(q, k, v, dist, valid, sinks):
    s = jnp.einsum('...qkgd,...skd->...kgqs', q, k).astype(jnp.float32) * (HEAD_DIM ** -0.5)
    slopes = alibi_slopes().reshape(N_KV_HEADS, Q_GROUP, 1, 1)
    s = s - slopes * dist[..., None, None, :, :].astype(jnp.float32)
    s = jnp.where(valid[..., None, None, :, :], s, -1e30)
    sink = jnp.broadcast_to(sinks.astype(jnp.float32).reshape(N_KV_HEADS, Q_GROUP, 1, 1), s.shape[:-1] + (1,))
    p = jax.nn.softmax(jnp.concatenate([s, sink], axis=-1), axis=-1)[..., :-1]
    return jnp.einsum('...kgqs,...skd->...qkgd', p.astype(v.dtype), v)


def attn_prompt(q, k, v, sinks):
    B, S = q.shape[:2]
    nb = S // ATT_BLOCK
    qb = q.reshape(B, nb, ATT_BLOCK, N_KV_HEADS, Q_GROUP, HEAD_DIM)

    def band(t):
        tp = jnp.pad(t, ((0, 0), (ATT_BLOCK, 0), (0, 0), (0, 0)))
        tp = tp.reshape(B, nb + 1, ATT_BLOCK, N_KV_HEADS, HEAD_DIM)
        return jnp.concatenate([tp[:, :-1], tp[:, 1:]], axis=2)

    qi = jnp.arange(ATT_BLOCK)[:, None]
    kj = jnp.arange(2 * ATT_BLOCK)[None, :]
    dist = qi + ATT_BLOCK - kj
    key_pos = (jnp.arange(nb)[:, None, None] - 1) * ATT_BLOCK + kj[None]
    valid = (dist >= 0) & (dist <= WINDOW) & (key_pos >= 0)
    o = sink_attention(qb, band(k), band(v), dist, valid, sinks)
    return o.reshape(B, S, ATT_W)


def attn_sample(q, k, v, ck, cv, sinks):
    Bd, Sd = q.shape[:2]
    W = ck.shape[1]
    kk = jnp.concatenate([ck, k], axis=1)
    vv = jnp.concatenate([cv, v], axis=1)
    s_rel = jnp.arange(W + Sd)[None, :] - W
    dist = jnp.arange(Sd)[:, None] - s_rel
    valid = (dist >= 0) & (dist <= WINDOW)
    o = sink_attention(q.reshape(Bd, Sd, N_KV_HEADS, Q_GROUP, HEAD_DIM), kk, vv, dist, valid, sinks)
    return o.reshape(Bd, Sd, ATT_W), kk[:, -W:], vv[:, -W:]


def multiscale_pool(u_ext, n_hist, w_pool, pool_scale):
    B, L, _ = u_ext.shape
    uf = u_ext.astype(jnp.float32).reshape(B, L, POOL_GROUPS, POOL_GC)
    cs = jnp.concatenate([jnp.zeros_like(uf[:, :1]), jnp.cumsum(uf, axis=1)], axis=1)
    hi = jnp.arange(n_hist, L) + 1
    win = jnp.asarray(POOL_WINDOWS, dtype=jnp.int32)
    lo = jnp.maximum(hi[:, None] - win[None, :], 0)
    grp = jnp.arange(POOL_GROUPS)[None, :]
    total = cs[:, hi] - cs[:, lo, grp]
    mean = total / (hi[:, None] - lo).astype(jnp.float32)[None, :, :, None]
    d = (mean - uf[:, n_hist:]).astype(u_ext.dtype)
    y = jnp.einsum('btgc,gce->btge', d, w_pool) * pool_scale.reshape(POOL_GROUPS, POOL_GC)
    return y.reshape(B, L - n_hist, POOL_W)


def peer_ffn(h, w_q, sub_keys, u_tab, v_tab):
    B, S, D = h.shape
    T = B * S
    hf = h.reshape(T, D)
    q = (hf @ w_q).reshape(T, PEER_HEADS, 2, PEER_DQ // 2)
    sc = jnp.einsum('thcd,hcnd->thcn', q, sub_keys).astype(jnp.float32)
    s_half, i_half = lax.top_k(sc, PEER_TOPK)
    cand = (s_half[:, :, 0, :, None] + s_half[:, :, 1, None, :]).reshape(T, PEER_HEADS, PEER_TOPK * PEER_TOPK)
    best, pos = lax.top_k(cand, PEER_TOPK)
    ia = jnp.take_along_axis(i_half[:, :, 0], pos // PEER_TOPK, axis=-1)
    ib = jnp.take_along_axis(i_half[:, :, 1], pos % PEER_TOPK, axis=-1)
    expert = ia * PEER_NKEYS + ib
    gate = jax.nn.softmax(best, axis=-1).astype(h.dtype)
    pad = (-T) % PEER_BLOCK
    nblk = (T + pad) // PEER_BLOCK

    def padr(a):
        a = jnp.pad(a, ((0, pad),) + ((0, 0),) * (a.ndim - 1))
        return a.reshape((nblk, PEER_BLOCK) + a.shape[1:])

    def block(args):
        hb, eb, gb = args
        act = jax.nn.gelu(jnp.einsum('td,thkd->thk', hb, u_tab[eb]))
        return jnp.einsum('thk,thkd->td', gb * act, v_tab[eb])

    out = lax.map(block, (padr(hf), padr(expert), padr(gate)))
    return out.reshape(nblk * PEER_BLOCK, D)[:T].reshape(B, S, D)


def mixer_projections(x, c, norm1_g, w_ada, b_ada, w_in, q_norm_g, k_norm_g):
    sh1, sc1, g1, sh2, sc2, g2 = adaln(c, w_ada, b_ada)
    h = rms_norm(x, norm1_g) * (1 + sc1) + sh1
    z = h @ w_in
    cuts = [ATT_W, ATT_W + KV_W, ATT_W + 2 * KV_W, ATT_W + 2 * KV_W + POOL_W, ATT_W + 2 * KV_W + POOL_W + D_MODEL]
    q, k, v, p, ga, gb = jnp.split(z, cuts, axis=-1)
    B, S = x.shape[:2]
    q = rms_norm(q.reshape(B, S, N_HEADS, HEAD_DIM), q_norm_g)
    k = rms_norm(k.reshape(B, S, N_KV_HEADS, HEAD_DIM), k_norm_g)
    v = v.reshape(B, S, N_KV_HEADS, HEAD_DIM)
    return q, k, v, p, ga, gb, (g1, sh2, sc2, g2)


def merge_and_channel_mix(x, attn_o, pool_o, ga, gb, mods, w_branch_attn, w_branch_pool, w_out,
                          norm2_g, peer_wq, peer_subkeys, peer_u, peer_v):
    g1, sh2, sc2, g2 = mods
    merged = jax.nn.sigmoid(ga) * (attn_o @ w_branch_attn) + jax.nn.sigmoid(gb) * (pool_o @ w_branch_pool)
    x = x + g1 * (merged @ w_out)
    h2 = rms_norm(x, norm2_g) * (1 + sc2) + sh2
    return x + g2 * peer_ffn(h2, peer_wq, peer_subkeys, peer_u, peer_v)


def setup_inputs(seed: int = 0):
    key = jax.random.key(seed)
    ks = jax.random.split(key, 24)

    def nrm(k, shape, scale=1.0):
        return jax.random.normal(k, shape, jnp.float32) * scale

    win_buf = min(WINDOW, PAST_LEN)
    L = DEPTH
    return {
        'x_prompt': nrm(ks[0], (BATCH, SEQ, D_MODEL)),
        'x_sample': nrm(ks[1], (DEC_BATCH, DEC_SEQ, D_MODEL)),
        'cache_k': nrm(ks[2], (L, DEC_BATCH, win_buf, N_KV_HEADS, HEAD_DIM)),
        'cache_v': nrm(ks[3], (L, DEC_BATCH, win_buf, N_KV_HEADS, HEAD_DIM)),
        'state_pool': nrm(ks[4], (L, DEC_BATCH, POOL_HIST, POOL_W)),
        'c_prompt': nrm(ks[5], (BATCH, D_MODEL)),
        'c_sample': nrm(ks[6], (DEC_BATCH, D_MODEL)),
        'norm1_g': 1.0 + nrm(ks[7], (L, D_MODEL), 0.1),
        'norm2_g': 1.0 + nrm(ks[8], (L, D_MODEL), 0.1),
        'w_ada': nrm(ks[9], (L, D_MODEL, 6 * D_MODEL), 0.5 * D_MODEL ** -0.5),
        'b_ada': nrm(ks[10], (L, 6 * D_MODEL), 0.02),
        'w_in': nrm(ks[11], (L, D_MODEL, IN_W), D_MODEL ** -0.5),
        'q_norm_g': 1.0 + nrm(ks[12], (L, HEAD_DIM), 0.1),
        'k_norm_g': 1.0 + nrm(ks[13], (L, HEAD_DIM), 0.1),
        'attn_sinks': nrm(ks[14], (L, N_HEADS), 0.5),
        'w_pool': nrm(ks[15], (L, POOL_GROUPS, POOL_GC, POOL_GC), POOL_GC ** -0.5),
        'pool_scale': 1.0 + nrm(ks[16], (L, POOL_W), 0.1),
        'w_branch_attn': nrm(ks[17], (L, ATT_W, D_MODEL), ATT_W ** -0.5),
        'w_branch_pool': nrm(ks[18], (L, POOL_W, D_MODEL), POOL_W ** -0.5),
        'w_out': nrm(ks[19], (L, D_MODEL, D_MODEL), D_MODEL ** -0.5),
        'peer_wq': nrm(ks[20], (L, D_MODEL, PEER_HEADS * PEER_DQ), D_MODEL ** -0.5),
        'peer_subkeys': nrm(ks[21], (L, PEER_HEADS, 2, PEER_NKEYS, PEER_DQ // 2), (PEER_DQ // 2) ** -0.5),
        'peer_u': nrm(ks[22], (L, PEER_EXPERTS, D_MODEL), D_MODEL ** -0.5),
        'peer_v': nrm(ks[23], (L, PEER_EXPERTS, D_MODEL), PEER_HEADS ** -0.5),
    }


def reference(x_prompt, x_sample, cache_k, cache_v, state_pool, c_prompt, c_sample,
              norm1_g, norm2_g, w_ada, b_ada, w_in, q_norm_g, k_norm_g, attn_sinks,
              w_pool, pool_scale, w_branch_attn, w_branch_pool, w_out,
              peer_wq, peer_subkeys, peer_u, peer_v):
    yp, ys = x_prompt, x_sample
    kp, vp, pp, ksm, vsm, psm = [], [], [], [], [], []
    n_hist = state_pool.shape[2]
    for l in range(DEPTH):
        q, k, v, p, ga, gb, mods = mixer_projections(yp, c_prompt, norm1_g[l], w_ada[l], b_ada[l],
                                                     w_in[l], q_norm_g[l], k_norm_g[l])
        ao = attn_prompt(q, k, v, attn_sinks[l])
        po = multiscale_pool(p, 0, w_pool[l], pool_scale[l])
        kp.append(k[:, -WINDOW:])
        vp.append(v[:, -WINDOW:])
        pp.append(p[:, -POOL_HIST:])
        yp = merge_and_channel_mix(yp, ao, po, ga, gb, mods, w_branch_attn[l], w_branch_pool[l], w_out[l],
                                   norm2_g[l], peer_wq[l], peer_subkeys[l], peer_u[l], peer_v[l])
        q, k, v, p, ga, gb, mods = mixer_projections(ys, c_sample, norm1_g[l], w_ada[l], b_ada[l],
                                                     w_in[l], q_norm_g[l], k_norm_g[l])
        ao, nk, nv = attn_sample(q, k, v, cache_k[l], cache_v[l], attn_sinks[l])
        p_ext = jnp.concatenate([state_pool[l], p], axis=1)
        po = multiscale_pool(p_ext, n_hist, w_pool[l], pool_scale[l])
        ksm.append(nk)
        vsm.append(nv)
        psm.append(p_ext[:, -n_hist:])
        ys = merge_and_channel_mix(ys, ao, po, ga, gb, mods, w_branch_attn[l], w_branch_pool[l], w_out[l],
                                   norm2_g[l], peer_wq[l], peer_subkeys[l], peer_u[l], peer_v[l])
    return (yp, ys, jnp.stack(kp), jnp.stack(vp), jnp.stack(pp), jnp.stack(ksm), jnp.stack(vsm), jnp.stack(psm))
```

```python
import functools

import jax
import jax.numpy as jnp
from jax import lax
from jax.experimental import pallas as pl
from jax.experimental.pallas import tpu as pltpu

F32 = jnp.float32
BF16 = jnp.bfloat16
NORM_EPS = 1e-6
N_HEADS = 8
N_KV_HEADS = 2
Q_GROUP = N_HEADS // N_KV_HEADS
HEAD_DIM = 64
WINDOW = 128
ATT_W = N_HEADS * HEAD_DIM
KV_W = N_KV_HEADS * HEAD_DIM
POOL_WINDOWS = (2, 4, 8, 16)
POOL_GC = 128
POOL_W = POOL_GC * len(POOL_WINDOWS)
POOL_HIST = max(POOL_WINDOWS) - 1
HIST_ROWS = 16
PEER_HEADS = 8
PEER_NKEYS = 128
PEER_TOPK = 16
PEER_SLOTS = PEER_HEADS * PEER_TOPK
PEER_TOKENS = 8
LANES = 128
VMEM_LIMIT_BYTES = 48 * 1024 * 1024
NEG_INF = float("-inf")
NT_DIMS = (((1,), (1,)), ((), ()))


def _params(n_axes):
    return pltpu.CompilerParams(dimension_semantics=("arbitrary",) * n_axes,
                                vmem_limit_bytes=VMEM_LIMIT_BYTES)


def _rms(x, g):
    return x * lax.rsqrt(jnp.mean(x * x, axis=-1, keepdims=True) + NORM_EPS) * g


def _bdot(a, b):
    return jnp.dot(a.astype(BF16), b.astype(BF16), preferred_element_type=F32)


def _bdot_nt(a, b):
    return lax.dot_general(a.astype(BF16), b.astype(BF16), NT_DIMS, preferred_element_type=F32)


def _ada_kernel(c_ref, w_ref, b_ref, o_ref):
    c = c_ref[...]
    o_ref[...] = _bdot(c * jax.nn.sigmoid(c), w_ref[...]) + b_ref[...]


def _ada(c, w_bf, b):
    n, d = c.shape
    nout = w_bf.shape[1]
    tn = nout // 4
    return pl.pallas_call(
        _ada_kernel, grid=(nout // tn,),
        in_specs=[pl.BlockSpec((n, d), lambda j: (0, 0)),
                  pl.BlockSpec((d, tn), lambda j: (0, j)),
                  pl.BlockSpec((1, tn), lambda j: (0, j))],
        out_specs=pl.BlockSpec((n, tn), lambda j: (0, j)),
        out_shape=jax.ShapeDtypeStruct((n, nout), F32),
        compiler_params=_params(1), name="ada")(c, w_bf, b.reshape(1, nout))


_IN_CUTS = (0, ATT_W, ATT_W + KV_W, ATT_W + 2 * KV_W, ATT_W + 2 * KV_W + POOL_W)


def _in_kernel(x_ref, sc_ref, sh_ref, g_ref, w_ref, q_ref, k_ref, v_ref, p_ref, ga_ref, gb_ref):
    d = x_ref.shape[-1]
    h = _rms(x_ref[0], g_ref[...]) * (1.0 + sc_ref[0]) + sh_ref[0]
    hb = h.astype(BF16)
    cuts = _IN_CUTS + (_IN_CUTS[-1] + d, _IN_CUTS[-1] + 2 * d)
    for o_ref, lo, hi in zip((q_ref, k_ref, v_ref, p_ref, ga_ref, gb_ref), cuts[:-1], cuts[1:]):
        o_ref[0] = jnp.dot(hb, w_ref[:, lo:hi], preferred_element_type=F32)


def _mod_spec(mod, tm):
    if mod.shape[1] == 1:
        return pl.BlockSpec((1, 1, mod.shape[2]), lambda b, j: (b, 0, 0))
    return pl.BlockSpec((1, tm, mod.shape[2]), lambda b, j: (b, j, 0))


def _tile(s):
    return min(256, s)


def _in_proj(x, sc, sh, g, w_bf):
    bsz, s, d = x.shape
    tm = _tile(s)
    widths = (ATT_W, KV_W, KV_W, POOL_W, d, d)
    tok = lambda w: pl.BlockSpec((1, tm, w), lambda b, j: (b, j, 0))
    return pl.pallas_call(
        _in_kernel, grid=(bsz, s // tm),
        in_specs=[tok(d), _mod_spec(sc, tm), _mod_spec(sh, tm),
                  pl.BlockSpec((1, d), lambda b, j: (0, 0)),
                  pl.BlockSpec(w_bf.shape, lambda b, j: (0, 0))],
        out_specs=[tok(w) for w in widths],
        out_shape=[jax.ShapeDtypeStruct((bsz, s, w), F32) for w in widths],
        compiler_params=_params(2), name="in_proj")(x, sc, sh, g.reshape(1, d), w_bf)


def _attn_p_kernel(sinks_ref, q_ref, kc_ref, kp_ref, vc_ref, vp_ref, qg_ref, kg_ref, o_ref, kn_ref):
    j = pl.program_id(1)
    blk = q_ref.shape[1]
    qi = lax.broadcasted_iota(jnp.int32, (blk, 2 * blk), 0)
    kj = lax.broadcasted_iota(jnp.int32, (blk, 2 * blk), 1)
    dist = qi + blk - kj
    valid = (dist >= 0) & (dist <= WINDOW) & ((kj >= blk) | (j > 0))
    dist_f = dist.astype(F32)
    qg = qg_ref[...]
    kg = kg_ref[...]
    for kv in range(N_KV_HEADS):
        ks = slice(kv * HEAD_DIM, (kv + 1) * HEAD_DIM)
        kcn = _rms(kc_ref[0, :, ks], kg)
        kn_ref[0, :, ks] = kcn
        kk = jnp.concatenate([_rms(kp_ref[0, :, ks], kg), kcn], axis=0).astype(BF16)
        vv = jnp.concatenate([vp_ref[0, :, ks], vc_ref[0, :, ks]], axis=0).astype(BF16)
        for g in range(Q_GROUP):
            h = kv * Q_GROUP + g
            hs = slice(h * HEAD_DIM, (h + 1) * HEAD_DIM)
            qn = _rms(q_ref[0, :, hs], qg)
            s = _bdot_nt(qn, kk) * (HEAD_DIM ** -0.5)
            s = s - (2.0 ** -(h + 1)) * dist_f
            s = jnp.where(valid, s, -1e30)
            sink = sinks_ref[h]
            m = jnp.maximum(jnp.max(s, axis=-1, keepdims=True), sink)
            e = jnp.exp(s - m)
            den = jnp.sum(e, axis=-1, keepdims=True) + jnp.exp(sink - m)
            o_ref[0, :, hs] = jnp.dot((e / den).astype(BF16), vv, preferred_element_type=F32)


def _attn_prompt(q, k, v, qg, kg, sinks):
    bsz, s, _ = q.shape
    blk = WINDOW
    cur = lambda w: pl.BlockSpec((1, blk, w), lambda b, j: (b, j, 0))
    prev = lambda w: pl.BlockSpec((1, blk, w), lambda b, j: (b, jnp.maximum(j - 1, 0), 0))
    vec = pl.BlockSpec((1, HEAD_DIM), lambda b, j: (0, 0))
    return pl.pallas_call(
        _attn_p_kernel, grid=(bsz, s // blk),
        in_specs=[pl.BlockSpec(memory_space=pltpu.SMEM), cur(ATT_W), cur(KV_W), prev(KV_W),
                  cur(KV_W), prev(KV_W), vec, vec],
        out_specs=[cur(ATT_W), cur(KV_W)],
        out_shape=[jax.ShapeDtypeStruct((bsz, s, ATT_W), F32),
                   jax.ShapeDtypeStruct((bsz, s, KV_W), F32)],
        compiler_params=_params(2), name="attn_prompt")(
            sinks, q, k, k, v, v, qg.reshape(1, HEAD_DIM), kg.reshape(1, HEAD_DIM))


def _row_consts(vals):
    row = lax.broadcasted_iota(jnp.int32, (N_HEADS, 1), 0)
    out = jnp.zeros((N_HEADS, 1), F32)
    for h, v in enumerate(vals):
        out = jnp.where(row == h, v, out)
    return out


def _attn_s_kernel(sinks_ref, q_ref, kx_ref, vx_ref, ck_ref, cv_ref, qg_ref, kg_ref, o_ref, kn_ref):
    nb = q_ref.shape[0]
    w = ck_ref.shape[1]
    slope = _row_consts([2.0 ** -(h + 1) for h in range(N_HEADS)])
    sink = _row_consts([sinks_ref[h] for h in range(N_HEADS)])
    row = lax.broadcasted_iota(jnp.int32, (N_HEADS, 1), 0)
    first = row < Q_GROUP
    dist_c = (w - lax.broadcasted_iota(jnp.int32, (1, w), 1)).astype(F32)
    scale = HEAD_DIM ** -0.5
    for b in range(nb):
        qn = _rms(q_ref[b], qg_ref[...])
        kxn = _rms(kx_ref[b], kg_ref[...])
        kn_ref[b] = kxn
        vx = vx_ref[b]
        ck = ck_ref[b]
        cv = cv_ref[b]
        kx_h = jnp.where(first, kxn[0:1], kxn[1:2])
        vx_h = jnp.where(first, vx[0:1], vx[1:2])
        s_c = jnp.where(first, _bdot_nt(qn, ck[:, :HEAD_DIM]), _bdot_nt(qn, ck[:, HEAD_DIM:]))
        s_c = s_c * scale - slope * dist_c
        s_n = jnp.sum(qn * kx_h, axis=-1, keepdims=True) * scale
        m = jnp.maximum(jnp.maximum(jnp.max(s_c, axis=-1, keepdims=True), s_n), sink)
        e_c = jnp.exp(s_c - m)
        e_n = jnp.exp(s_n - m)
        den = jnp.sum(e_c, axis=-1, keepdims=True) + e_n + jnp.exp(sink - m)
        p_c = e_c / den
        o_c = jnp.where(first, _bdot(p_c, cv[:, :HEAD_DIM]), _bdot(p_c, cv[:, HEAD_DIM:]))
        o_ref[b] = o_c + (e_n / den) * vx_h


def _attn_sample(q, kx, vx, ck, cv, qg, kg, sinks):
    n = q.shape[0]
    w = ck.shape[1]
    nb = 8
    blk = lambda a, c: pl.BlockSpec((nb, a, c), lambda i: (i, 0, 0))
    vec = pl.BlockSpec((1, HEAD_DIM), lambda i: (0, 0))
    return pl.pallas_call(
        _attn_s_kernel, grid=(n // nb,),
        in_specs=[pl.BlockSpec(memory_space=pltpu.SMEM), blk(N_HEADS, HEAD_DIM),
                  blk(N_KV_HEADS, HEAD_DIM), blk(N_KV_HEADS, HEAD_DIM), blk(w, KV_W), blk(w, KV_W),
                  vec, vec],
        out_specs=[blk(N_HEADS, HEAD_DIM), blk(N_KV_HEADS, HEAD_DIM)],
        out_shape=[jax.ShapeDtypeStruct((n, N_HEADS, HEAD_DIM), F32),
                   jax.ShapeDtypeStruct((n, N_KV_HEADS, HEAD_DIM), F32)],
        compiler_params=_params(1), name="attn_sample")(
            sinks, q, kx, vx, ck, cv, qg.reshape(1, HEAD_DIM), kg.reshape(1, HEAD_DIM))


def _pool_p_kernel(p_ref, hist_ref, d_ref, ext_ref):
    j = pl.program_id(1)
    tm = p_ref.shape[1]
    cur = p_ref[0]
    ext_ref[0:HIST_ROWS, :] = hist_ref[0, 0]
    ext_ref[HIST_ROWS:, :] = cur
    pos = j * tm + lax.broadcasted_iota(jnp.int32, (tm, 1), 0)
    acc = cur
    lo = 1
    for gi, w in enumerate(POOL_WINDOWS):
        for s in range(lo, w):
            acc = acc + ext_ref[HIST_ROWS - s:HIST_ROWS - s + tm, :]
        lo = w
        gs = slice(gi * POOL_GC, (gi + 1) * POOL_GC)
        cnt = jnp.minimum(pos + 1, w).astype(F32)
        d_ref[0, :, gs] = acc[:, gs] / cnt - cur[:, gs]


def _pool_prompt(p):
    bsz, s, pw = p.shape
    tm = _tile(s)
    nt = s // tm
    tails = p.reshape(bsz, nt, tm, pw)[:, :, tm - HIST_ROWS:, :]
    hist = jnp.concatenate([jnp.zeros((bsz, 1, HIST_ROWS, pw), F32), tails[:, :-1]], axis=1)
    return pl.pallas_call(
        _pool_p_kernel, grid=(bsz, nt),
        in_specs=[pl.BlockSpec((1, tm, pw), lambda b, j: (b, j, 0)),
                  pl.BlockSpec((1, 1, HIST_ROWS, pw), lambda b, j: (b, j, 0, 0))],
        out_specs=pl.BlockSpec((1, tm, pw), lambda b, j: (b, j, 0)),
        out_shape=jax.ShapeDtypeStruct((bsz, s, pw), F32),
        scratch_shapes=[pltpu.VMEM((tm + HIST_ROWS, pw), F32)],
        compiler_params=_params(2), name="pool_prompt")(p, hist)


def _pool_s_kernel(p_ref, st_ref, d_ref):
    n_hist = st_ref.shape[1]
    cur = p_ref[...]
    acc = cur
    lo = 1
    for gi, w in enumerate(POOL_WINDOWS):
        for s in range(lo, w):
            acc = acc + st_ref[:, n_hist - s, :]
        lo = w
        gs = slice(gi * POOL_GC, (gi + 1) * POOL_GC)
        d_ref[:, gs] = acc[:, gs] / float(w) - cur[:, gs]


def _pool_sample(p, state):
    assert state.shape[1] == POOL_HIST
    return pl.pallas_call(
        _pool_s_kernel, out_shape=jax.ShapeDtypeStruct(p.shape, F32),
        compiler_params=pltpu.CompilerParams(vmem_limit_bytes=VMEM_LIMIT_BYTES),
        name="pool_sample")(p, state)


def _mix_kernel(x_ref, ao_ref, d_ref, ga_ref, gb_ref, g1_ref, sc2_ref, sh2_ref, n2_ref,
                wp_ref, ps_ref, wba_ref, wbp_ref, wo_ref, x1_ref, h2_ref):
    d = d_ref[0]
    po = jnp.concatenate(
        [_bdot(d[:, gi * POOL_GC:(gi + 1) * POOL_GC], wp_ref[gi]) for gi in range(len(POOL_WINDOWS))],
        axis=-1) * ps_ref[...]
    merged = (jax.nn.sigmoid(ga_ref[0]) * _bdot(ao_ref[0], wba_ref[...])
              + jax.nn.sigmoid(gb_ref[0]) * _bdot(po, wbp_ref[...]))
    x1 = x_ref[0] + g1_ref[0] * _bdot(merged, wo_ref[...])
    x1_ref[0] = x1
    h2_ref[0] = _rms(x1, n2_ref[...]) * (1.0 + sc2_ref[0]) + sh2_ref[0]


def _mix(x, ao, d, ga, gb, g1, sc2, sh2, n2g, wp_bf, pscale, wba_bf, wbp_bf, wo_bf):
    bsz, s, dm = x.shape
    tm = _tile(s)
    tok = lambda w: pl.BlockSpec((1, tm, w), lambda b, j: (b, j, 0))
    full = lambda a: pl.BlockSpec(a.shape, lambda b, j: (0,) * a.ndim)
    n2g = n2g.reshape(1, dm)
    pscale = pscale.reshape(1, POOL_W)
    return pl.pallas_call(
        _mix_kernel, grid=(bsz, s // tm),
        in_specs=[tok(dm), tok(ATT_W), tok(POOL_W), tok(dm), tok(dm),
                  _mod_spec(g1, tm), _mod_spec(sc2, tm), _mod_spec(sh2, tm), full(n2g),
                  full(wp_bf), full(pscale), full(wba_bf), full(wbp_bf), full(wo_bf)],
        out_specs=[tok(dm), tok(dm)],
        out_shape=[jax.ShapeDtypeStruct((bsz, s, dm), F32)] * 2,
        compiler_params=_params(2), name="mix")(
            x, ao, d, ga, gb, g1, sc2, sh2, n2g, wp_bf, pscale, wba_bf, wbp_bf, wo_bf)


def _route_kernel(h2_ref, wq_ref, sk_ref, eid_ref, gate_ref, sc_ref):
    tm = h2_ref.shape[0]
    hb = h2_ref[...].astype(BF16)
    for hc in range(2 * PEER_HEADS):
        q = jnp.dot(hb, wq_ref[:, hc * LANES:(hc + 1) * LANES], preferred_element_type=F32)
        sc_ref[hc] = _bdot_nt(q, sk_ref[hc])
    lane = lax.broadcasted_iota(jnp.int32, (tm, LANES), 1)
    lane_f = lane.astype(F32)
    lane_hi = lax.shift_right_logical(lane, 4)
    lane_lo = lane & (PEER_TOPK - 1)
    zeros = jnp.zeros((tm, LANES), F32)
    half = LANES // PEER_TOPK

    def take_max(s):
        m = jnp.max(s, axis=-1, keepdims=True)
        idx = jnp.min(jnp.where(s == m, lane_f, float(LANES)), axis=-1, keepdims=True)
        return m, idx, jnp.where(lane_f == idx, NEG_INF, s)

    def head_body(h, _):
        def first_half(i, c):
            s, v_lo, v_hi, i_lo, i_hi = c
            m, idx, s = take_max(s)
            at_lo = lane_hi == i
            at_hi = lane_hi == i - half
            return (s, jnp.where(at_lo, m, v_lo), jnp.where(at_hi, m, v_hi),
                    jnp.where(at_lo, idx, i_lo), jnp.where(at_hi, idx, i_hi))

        def second_half(j, c):
            s, v, ix = c
            m, idx, s = take_max(s)
            at = lane_lo == j
            return s, jnp.where(at, m, v), jnp.where(at, idx, ix)

        _, a_lo, a_hi, ia_lo, ia_hi = lax.fori_loop(
            0, PEER_TOPK, first_half, (sc_ref[2 * h], zeros, zeros, zeros, zeros))
        _, b_t, ib_t = lax.fori_loop(0, PEER_TOPK, second_half, (sc_ref[2 * h + 1], zeros, zeros))
        e_lo = ia_lo * float(PEER_NKEYS) + ib_t
        e_hi = ia_hi * float(PEER_NKEYS) + ib_t

        def pick(k, c):
            c_lo, c_hi, best, eids = c
            m = jnp.maximum(jnp.max(c_lo, axis=-1, keepdims=True), jnp.max(c_hi, axis=-1, keepdims=True))
            p_lo = jnp.min(jnp.where(c_lo == m, lane_f, 2.0 * LANES), axis=-1, keepdims=True)
            p_hi = jnp.min(jnp.where(c_hi == m, lane_f + LANES, 2.0 * LANES), axis=-1, keepdims=True)
            p = jnp.minimum(p_lo, p_hi)
            sel_lo = lane_f == p
            sel_hi = lane_f + LANES == p
            e = jnp.maximum(jnp.max(jnp.where(sel_lo, e_lo, -1.0), axis=-1, keepdims=True),
                            jnp.max(jnp.where(sel_hi, e_hi, -1.0), axis=-1, keepdims=True))
            at = lane == k
            return (jnp.where(sel_lo, NEG_INF, c_lo), jnp.where(sel_hi, NEG_INF, c_hi),
                    jnp.where(at, m, best), jnp.where(at, e, eids))

        _, _, best, eids = lax.fori_loop(0, PEER_TOPK, pick, (a_lo + b_t, a_hi + b_t, zeros, zeros))
        live = lane < PEER_TOPK
        mx = jnp.max(jnp.where(live, best, NEG_INF), axis=-1, keepdims=True)
        ex = jnp.where(live, jnp.exp(best - mx), 0.0)
        gate_ref[h] = ex / jnp.sum(ex, axis=-1, keepdims=True)
        eid_ref[h] = eids.astype(jnp.int32)
        return 0

    lax.fori_loop(0, PEER_HEADS, head_body, 0)


def _route(h2, wq_bf, sk_bf):
    t, dm = h2.shape
    tm = min(128, t)
    out = pl.BlockSpec((PEER_HEADS, tm, LANES), lambda i: (0, i, 0))
    eid, gate = pl.pallas_call(
        _route_kernel, grid=(t // tm,),
        in_specs=[pl.BlockSpec((tm, dm), lambda i: (i, 0)),
                  pl.BlockSpec(wq_bf.shape, lambda i: (0, 0)),
                  pl.BlockSpec(sk_bf.shape, lambda i: (0, 0, 0))],
        out_specs=[out, out],
        out_shape=[jax.ShapeDtypeStruct((PEER_HEADS, t, LANES), jnp.int32),
                   jax.ShapeDtypeStruct((PEER_HEADS, t, LANES), F32)],
        scratch_shapes=[pltpu.VMEM((2 * PEER_HEADS, tm, LANES), F32)],
        compiler_params=_params(1), name="route")(h2, wq_bf, sk_bf)
    flat = lambda a: jnp.transpose(a[:, :, :PEER_TOPK], (1, 0, 2)).reshape(t, PEER_SLOTS)
    return flat(eid), flat(gate)


def _gelu_tanh(x):
    return x * (0.5 * (1.0 + jnp.tanh(0.7978845608028654 * (x + 0.044715 * (x * x * x)))))


def _peer_kernel(eid_ref, h2_ref, gate_ref, x1_ref, g2_ref, u_hbm, v_hbm, o_ref, ubuf, vbuf, sem):
    n_tok = h2_ref.shape[0]

    def row_copies(t, k):
        e = eid_ref[t, k]
        r = t * PEER_SLOTS + k
        return (pltpu.make_async_copy(u_hbm.at[pl.ds(e, 1)], ubuf.at[pl.ds(r, 1)], sem.at[0]),
                pltpu.make_async_copy(v_hbm.at[pl.ds(e, 1)], vbuf.at[pl.ds(r, 1)], sem.at[1]))

    def start_row(r, _):
        cu, cv = row_copies(lax.shift_right_logical(r, 7), r & (PEER_SLOTS - 1))
        cu.start()
        cv.start()
        return 0

    def wait_row(r, _):
        cu, cv = row_copies(lax.shift_right_logical(r, 7), r & (PEER_SLOTS - 1))
        cu.wait()
        cv.wait()
        return 0

    lax.fori_loop(0, n_tok * PEER_SLOTS, start_row, 0)
    lax.fori_loop(0, n_tok * PEER_SLOTS, wait_row, 0)

    hb = h2_ref[...].astype(BF16)
    gate = gate_ref[...]
    row = lax.broadcasted_iota(jnp.int32, (n_tok, 1), 0)
    acc = jnp.zeros(o_ref.shape, F32)
    for t in range(n_tok):
        rows = pl.ds(t * PEER_SLOTS, PEER_SLOTS)
        act = _gelu_tanh(_bdot_nt(hb, ubuf[rows, :]))
        out_t = _bdot(gate * act, vbuf[rows, :])
        acc = jnp.where(row == t, out_t, acc)
    o_ref[...] = x1_ref[...] + g2_ref[:, 0, :] * acc


def _peer(eid, h2, gate, x1, g2, u_tab, v_tab, rows_per_mod):
    t, dm = h2.shape
    nt = PEER_TOKENS
    if rows_per_mod == 1:
        g2_spec = pl.BlockSpec((nt, 1, dm), lambda i: (i, 0, 0))
    else:
        assert rows_per_mod % nt == 0
        g2_spec = pl.BlockSpec((1, 1, dm), lambda i: ((i * nt) // rows_per_mod, 0, 0))
    tok = lambda w: pl.BlockSpec((nt, w), lambda i: (i, 0))
    return pl.pallas_call(
        _peer_kernel, grid=(t // nt,),
        in_specs=[pl.BlockSpec((nt, PEER_SLOTS), lambda i: (i, 0), memory_space=pltpu.SMEM),
                  tok(dm), tok(PEER_SLOTS), tok(dm), g2_spec,
                  pl.BlockSpec(memory_space=pl.ANY), pl.BlockSpec(memory_space=pl.ANY)],
        out_specs=tok(dm),
        out_shape=jax.ShapeDtypeStruct((t, dm), F32),
        scratch_shapes=[pltpu.VMEM((nt * PEER_SLOTS, dm), F32),
                        pltpu.VMEM((nt * PEER_SLOTS, dm), F32),
                        pltpu.SemaphoreType.DMA((2,))],
        compiler_params=_params(1), name="peer")(eid, h2, gate, x1, g2, u_tab, v_tab)


def _channel_mix(x, ao, d, ga, gb, mods, wts):
    g1, sh2, sc2, g2 = mods
    bsz, s, dm = x.shape
    x1, h2 = _mix(x, ao, d, ga, gb, g1, sc2, sh2, wts["norm2_g"], wts["w_pool"], wts["pool_scale"],
                  wts["w_branch_attn"], wts["w_branch_pool"], wts["w_out"])
    h2f = h2.reshape(bsz * s, dm)
    eid, gate = _route(h2f, wts["peer_wq"], wts["peer_subkeys"])
    per_token = g2.shape[1] != 1
    g2f = g2.reshape(-1, 1, dm)
    out = _peer(eid, h2f, gate, x1.reshape(bsz * s, dm), g2f, wts["peer_u"], wts["peer_v"],
                1 if per_token else s)
    return out.reshape(bsz, s, dm)


def kernel(x_prompt, x_sample, cache_k, cache_v, state_pool, c_prompt, c_sample, norm1_g, norm2_g, w_ada, b_ada, w_in, q_norm_g, k_norm_g, attn_sinks, w_pool, pool_scale, w_branch_attn, w_branch_pool, w_out, peer_wq, peer_subkeys, peer_u, peer_v):
    depth = w_ada.shape[0]
    bsz, seq, dm = x_prompt.shape
    nd = x_sample.shape[0]
    assert x_sample.shape[1] == 1
    yp = x_prompt
    ys = x_sample.reshape(1, nd, dm)
    kp, vp, pp, ksm, vsm, psm = [], [], [], [], [], []
    for l in range(depth):
        wts = {
            "norm2_g": norm2_g[l], "pool_scale": pool_scale[l],
            "w_pool": w_pool[l].astype(BF16), "w_branch_attn": w_branch_attn[l].astype(BF16),
            "w_branch_pool": w_branch_pool[l].astype(BF16), "w_out": w_out[l].astype(BF16),
            "peer_wq": peer_wq[l].astype(BF16),
            "peer_subkeys": peer_subkeys[l].reshape(2 * PEER_HEADS, PEER_NKEYS, -1).astype(BF16),
            "peer_u": peer_u[l], "peer_v": peer_v[l],
        }
        w_in_bf = w_in[l].astype(BF16)
        m = _ada(jnp.concatenate([c_prompt, c_sample], axis=0), w_ada[l].astype(BF16), b_ada[l])
        m6 = [m[:, i * dm:(i + 1) * dm] for i in range(6)]
        mp = [a[:bsz].reshape(bsz, 1, dm) for a in m6]
        ms = [a[bsz:].reshape(1, nd, dm) for a in m6]

        q, k, v, p, ga, gb = _in_proj(yp, mp[1], mp[0], norm1_g[l], w_in_bf)
        ao, kn = _attn_prompt(q, k, v, q_norm_g[l], k_norm_g[l], attn_sinks[l])
        kp.append(kn[:, -WINDOW:].reshape(bsz, WINDOW, N_KV_HEADS, HEAD_DIM))
        vp.append(v[:, -WINDOW:].reshape(bsz, WINDOW, N_KV_HEADS, HEAD_DIM))
        pp.append(p[:, -POOL_HIST:])
        yp = _channel_mix(yp, ao, _pool_prompt(p), ga, gb, (mp[2], mp[3], mp[4], mp[5]), wts)

        q, k, v, p, ga, gb = _in_proj(ys, ms[1], ms[0], norm1_g[l], w_in_bf)
        ck = cache_k[l]
        cv = cache_v[l]
        win = ck.shape[1]
        ao, kn = _attn_sample(q.reshape(nd, N_HEADS, HEAD_DIM), k.reshape(nd, N_KV_HEADS, HEAD_DIM),
                              v.reshape(nd, N_KV_HEADS, HEAD_DIM), ck.reshape(nd, win, KV_W),
                              cv.reshape(nd, win, KV_W), q_norm_g[l], k_norm_g[l], attn_sinks[l])
        ksm.append(jnp.concatenate([ck, kn[:, None]], axis=1)[:, -win:])
        vsm.append(jnp.concatenate([cv, v.reshape(nd, 1, N_KV_HEADS, HEAD_DIM)], axis=1)[:, -win:])
        st = state_pool[l]
        psm.append(jnp.concatenate([st, p.reshape(nd, 1, POOL_W)], axis=1)[:, -st.shape[1]:])
        d = _pool_sample(p.reshape(nd, POOL_W), st).reshape(1, nd, POOL_W)
        ys = _channel_mix(ys, ao.reshape(1, nd, ATT_W), d, ga, gb, (ms[2], ms[3], ms[4], ms[5]), wts)
    return (yp, ys.reshape(nd, 1, dm), jnp.stack(kp), jnp.stack(vp), jnp.stack(pp),
            jnp.stack(ksm), jnp.stack(vsm), jnp.stack(psm))
```

```python
import functools

import jax
import jax.numpy as jnp
from jax import lax
from jax.experimental import pallas as pl
from jax.experimental.pallas import tpu as pltpu

F32 = jnp.float32
BF16 = jnp.bfloat16
NORM_EPS = 1e-6
N_HEADS = 8
N_KV_HEADS = 2
Q_GROUP = N_HEADS // N_KV_HEADS
HEAD_DIM = 64
WINDOW = 128
ATT_W = N_HEADS * HEAD_DIM
KV_W = N_KV_HEADS * HEAD_DIM
POOL_WINDOWS = (2, 4, 8, 16)
POOL_GC = 128
POOL_W = POOL_GC * len(POOL_WINDOWS)
POOL_HIST = max(POOL_WINDOWS) - 1
HIST_ROWS = 16
PEER_HEADS = 8
PEER_NKEYS = 128
PEER_TOPK = 16
PEER_SLOTS = PEER_HEADS * PEER_TOPK
PEER_TOKENS = 8
LANES = 128
VMEM_LIMIT_BYTES = 48 * 1024 * 1024
NEG_INF = float("-inf")
NT_DIMS = (((1,), (1,)), ((), ()))


def _params(n_axes):
    return pltpu.CompilerParams(dimension_semantics=("arbitrary",) * n_axes,
                                vmem_limit_bytes=VMEM_LIMIT_BYTES)


def _rms(x, g):
    return x * lax.rsqrt(jnp.mean(x * x, axis=-1, keepdims=True) + NORM_EPS) * g


def _bdot(a, b):
    return jnp.dot(a.astype(BF16), b.astype(BF16), preferred_element_type=F32)


def _bdot_nt(a, b):
    return lax.dot_general(a.astype(BF16), b.astype(BF16), NT_DIMS, preferred_element_type=F32)


def _ada_kernel(c_ref, w_ref, b_ref, o_ref):
    c = c_ref[...]
    o_ref[...] = _bdot(c * jax.nn.sigmoid(c), w_ref[...]) + b_ref[...]


def _ada(c, w_bf, b):
    n, d = c.shape
    nout = w_bf.shape[1]
    tn = nout // 4
    return pl.pallas_call(
        _ada_kernel, grid=(nout // tn,),
        in_specs=[pl.BlockSpec((n, d), lambda j: (0, 0)),
                  pl.BlockSpec((d, tn), lambda j: (0, j)),
                  pl.BlockSpec((1, tn), lambda j: (0, j))],
        out_specs=pl.BlockSpec((n, tn), lambda j: (0, j)),
        out_shape=jax.ShapeDtypeStruct((n, nout), F32),
        compiler_params=_params(1), name="ada")(c, w_bf, b.reshape(1, nout))


_IN_CUTS = (0, ATT_W, ATT_W + KV_W, ATT_W + 2 * KV_W, ATT_W + 2 * KV_W + POOL_W)


def _in_kernel(x_ref, sc_ref, sh_ref, g_ref, w_ref, q_ref, k_ref, v_ref, p_ref, ga_ref, gb_ref):
    d = x_ref.shape[-1]
    h = _rms(x_ref[0], g_ref[...]) * (1.0 + sc_ref[0]) + sh_ref[0]
    hb = h.astype(BF16)
    cuts = _IN_CUTS + (_IN_CUTS[-1] + d, _IN_CUTS[-1] + 2 * d)
    for o_ref, lo, hi in zip((q_ref, k_ref, v_ref, p_ref, ga_ref, gb_ref), cuts[:-1], cuts[1:]):
        o_ref[0] = jnp.dot(hb, w_ref[:, lo:hi], preferred_element_type=F32)


def _mod_spec(mod, tm):
    if mod.shape[1] == 1:
        return pl.BlockSpec((1, 1, mod.shape[2]), lambda b, j: (b, 0, 0))
    return pl.BlockSpec((1, tm, mod.shape[2]), lambda b, j: (b, j, 0))


def _tile(s):
    return min(256, s)


def _in_proj(x, sc, sh, g, w_bf):
    bsz, s, d = x.shape
    tm = _tile(s)
    widths = (ATT_W, KV_W, KV_W, POOL_W, d, d)
    tok = lambda w: pl.BlockSpec((1, tm, w), lambda b, j: (b, j, 0))
    return pl.pallas_call(
        _in_kernel, grid=(bsz, s // tm),
        in_specs=[tok(d), _mod_spec(sc, tm), _mod_spec(sh, tm),
                  pl.BlockSpec((1, d), lambda b, j: (0, 0)),
                  pl.BlockSpec(w_bf.shape, lambda b, j: (0, 0))],
        out_specs=[tok(w) for w in widths],
        out_shape=[jax.ShapeDtypeStruct((bsz, s, w), F32) for w in widths],
        compiler_params=_params(2), name="in_proj")(x, sc, sh, g.reshape(1, d), w_bf)


def _attn_p_kernel(sinks_ref, q_ref, kc_ref, kp_ref, vc_ref, vp_ref, qg_ref, kg_ref, o_ref, kn_ref):
    j = pl.program_id(1)
    blk = q_ref.shape[1]
    qi = lax.broadcasted_iota(jnp.int32, (blk, 2 * blk), 0)
    kj = lax.broadcasted_iota(jnp.int32, (blk, 2 * blk), 1)
    dist = qi + blk - kj
    valid = (dist >= 0) & (dist <= WINDOW) & ((kj >= blk) | (j > 0))
    dist_f = dist.astype(F32)
    qg = qg_ref[...]
    kg = kg_ref[...]
    for kv in range(N_KV_HEADS):
        ks = slice(kv * HEAD_DIM, (kv + 1) * HEAD_DIM)
        kcn = _rms(kc_ref[0, :, ks], kg)
        kn_ref[0, :, ks] = kcn
        kk = jnp.concatenate([_rms(kp_ref[0, :, ks], kg), kcn], axis=0).astype(BF16)
        vv = jnp.concatenate([vp_ref[0, :, ks], vc_ref[0, :, ks]], axis=0).astype(BF16)
        for g in range(Q_GROUP):
            h = kv * Q_GROUP + g
            hs = slice(h * HEAD_DIM, (h + 1) * HEAD_DIM)
            qn = _rms(q_ref[0, :, hs], qg)
            s = _bdot_nt(qn, kk) * (HEAD_DIM ** -0.5)
            s = s - (2.0 ** -(h + 1)) * dist_f
            s = jnp.where(valid, s, -1e30)
            sink = sinks_ref[h]
            m = jnp.maximum(jnp.max(s, axis=-1, keepdims=True), sink)
            e = jnp.exp(s - m)
            den = jnp.sum(e, axis=-1, keepdims=True) + jnp.exp(sink - m)
            o_ref[0, :, hs] = jnp.dot((e / den).astype(BF16), vv, preferred_element_type=F32)


def _attn_prompt(q, k, v, qg, kg, sinks):
    bsz, s, _ = q.shape
    blk = WINDOW
    cur = lambda w: pl.BlockSpec((1, blk, w), lambda b, j: (b, j, 0))
    prev = lambda w: pl.BlockSpec((1, blk, w), lambda b, j: (b, jnp.maximum(j - 1, 0), 0))
    vec = pl.BlockSpec((1, HEAD_DIM), lambda b, j: (0, 0))
    return pl.pallas_call(
        _attn_p_kernel, grid=(bsz, s // blk),
        in_specs=[pl.BlockSpec(memory_space=pltpu.SMEM), cur(ATT_W), cur(KV_W), prev(KV_W),
                  cur(KV_W), prev(KV_W), vec, vec],
        out_specs=[cur(ATT_W), cur(KV_W)],
        out_shape=[jax.ShapeDtypeStruct((bsz, s, ATT_W), F32),
                   jax.ShapeDtypeStruct((bsz, s, KV_W), F32)],
        compiler_params=_params(2), name="attn_prompt")(
            sinks, q, k, k, v, v, qg.reshape(1, HEAD_DIM), kg.reshape(1, HEAD_DIM))


def _row_consts(vals):
    row = lax.broadcasted_iota(jnp.int32, (N_HEADS, 1), 0)
    out = jnp.zeros((N_HEADS, 1), F32)
    for h, v in enumerate(vals):
        out = jnp.where(row == h, v, out)
    return out


def _attn_s_kernel(sinks_ref, q_ref, kx_ref, vx_ref, ck_ref, cv_ref, qg_ref, kg_ref, o_ref, kn_ref):
    nb = q_ref.shape[0]
    w = ck_ref.shape[1]
    slope = _row_consts([2.0 ** -(h + 1) for h in range(N_HEADS)])
    sink = _row_consts([sinks_ref[h] for h in range(N_HEADS)])
    row = lax.broadcasted_iota(jnp.int32, (N_HEADS, 1), 0)
    first = row < Q_GROUP
    dist_c = (w - lax.broadcasted_iota(jnp.int32, (1, w), 1)).astype(F32)
    scale = HEAD_DIM ** -0.5
    for b in range(nb):
        qn = _rms(q_ref[b], qg_ref[...])
        kxn = _rms(kx_ref[b], kg_ref[...])
        kn_ref[b] = kxn
        vx = vx_ref[b]
        ck = ck_ref[b]
        cv = cv_ref[b]
        kx_h = jnp.where(first, kxn[0:1], kxn[1:2])
        vx_h = jnp.where(first, vx[0:1], vx[1:2])
        s_c = jnp.where(first, _bdot_nt(qn, ck[:, :HEAD_DIM]), _bdot_nt(qn, ck[:, HEAD_DIM:]))
        s_c = s_c * scale - slope * dist_c
        s_n = jnp.sum(qn * kx_h, axis=-1, keepdims=True) * scale
        m = jnp.maximum(jnp.maximum(jnp.max(s_c, axis=-1, keepdims=True), s_n), sink)
        e_c = jnp.exp(s_c - m)
        e_n = jnp.exp(s_n - m)
        den = jnp.sum(e_c, axis=-1, keepdims=True) + e_n + jnp.exp(sink - m)
        p_c = e_c / den
        o_c = jnp.where(first, _bdot(p_c, cv[:, :HEAD_DIM]), _bdot(p_c, cv[:, HEAD_DIM:]))
        o_ref[b] = o_c + (e_n / den) * vx_h


def _attn_sample(q, kx, vx, ck, cv, qg, kg, sinks):
    n = q.shape[0]
    w = ck.shape[1]
    nb = 8
    blk = lambda a, c: pl.BlockSpec((nb, a, c), lambda i: (i, 0, 0))
    vec = pl.BlockSpec((1, HEAD_DIM), lambda i: (0, 0))
    return pl.pallas_call(
        _attn_s_kernel, grid=(n // nb,),
        in_specs=[pl.BlockSpec(memory_space=pltpu.SMEM), blk(N_HEADS, HEAD_DIM),
                  blk(N_KV_HEADS, HEAD_DIM), blk(N_KV_HEADS, HEAD_DIM), blk(w, KV_W), blk(w, KV_W),
                  vec, vec],
        out_specs=[blk(N_HEADS, HEAD_DIM), blk(N_KV_HEADS, HEAD_DIM)],
        out_shape=[jax.ShapeDtypeStruct((n, N_HEADS, HEAD_DIM), F32),
                   jax.ShapeDtypeStruct((n, N_KV_HEADS, HEAD_DIM), F32)],
        compiler_params=_params(1), name="attn_sample")(
            sinks, q, kx, vx, ck, cv, qg.reshape(1, HEAD_DIM), kg.reshape(1, HEAD_DIM))


def _pool_p_kernel(p_ref, hist_ref, d_ref, ext_ref):
    j = pl.program_id(1)
    tm = p_ref.shape[1]
    cur = p_ref[0]
    ext_ref[0:HIST_ROWS, :] = hist_ref[0, 0]
    ext_ref[HIST_ROWS:, :] = cur
    pos = j * tm + lax.broadcasted_iota(jnp.int32, (tm, 1), 0)
    acc = cur
    lo = 1
    for gi, w in enumerate(POOL_WINDOWS):
        for s in range(lo, w):
            acc = acc + ext_ref[HIST_ROWS - s:HIST_ROWS - s + tm, :]
        lo = w
        gs = slice(gi * POOL_GC, (gi + 1) * POOL_GC)
        cnt = jnp.minimum(pos + 1, w).astype(F32)
        d_ref[0, :, gs] = acc[:, gs] / cnt - cur[:, gs]


def _pool_prompt(p):
    bsz, s, pw = p.shape
    tm = _tile(s)
    nt = s // tm
    tails = p.reshape(bsz, nt, tm, pw)[:, :, tm - HIST_ROWS:, :]
    hist = jnp.concatenate([jnp.zeros((bsz, 1, HIST_ROWS, pw), F32), tails[:, :-1]], axis=1)
    return pl.pallas_call(
        _pool_p_kernel, grid=(bsz, nt),
        in_specs=[pl.BlockSpec((1, tm, pw), lambda b, j: (b, j, 0)),
                  pl.BlockSpec((1, 1, HIST_ROWS, pw), lambda b, j: (b, j, 0, 0))],
        out_specs=pl.BlockSpec((1, tm, pw), lambda b, j: (b, j, 0)),
        out_shape=jax.ShapeDtypeStruct((bsz, s, pw), F32),
        scratch_shapes=[pltpu.VMEM((tm + HIST_ROWS, pw), F32)],
        compiler_params=_params(2), name="pool_prompt")(p, hist)


def _pool_s_kernel(p_ref, st_ref, d_ref):
    n_hist = st_ref.shape[1]
    cur = p_ref[...]
    acc = cur
    lo = 1
    for gi, w in enumerate(POOL_WINDOWS):
        for s in range(lo, w):
            acc = acc + st_ref[:, n_hist - s, :]
        lo = w
        gs = slice(gi * POOL_GC, (gi + 1) * POOL_GC)
        d_ref[:, gs] = acc[:, gs] / float(w) - cur[:, gs]


def _pool_sample(p, state):
    assert state.shape[1] == POOL_HIST
    return pl.pallas_call(
        _pool_s_kernel, out_shape=jax.ShapeDtypeStruct(p.shape, F32),
        compiler_params=pltpu.CompilerParams(vmem_limit_bytes=VMEM_LIMIT_BYTES),
        name="pool_sample")(p, state)


def _mix_kernel(x_ref, ao_ref, d_ref, ga_ref, gb_ref, g1_ref, sc2_ref, sh2_ref, n2_ref,
                wp_ref, ps_ref, wba_ref, wbp_ref, wo_ref, x1_ref, h2_ref):
    d = d_ref[0]
    po = jnp.concatenate(
        [_bdot(d[:, gi * POOL_GC:(gi + 1) * POOL_GC], wp_ref[gi]) for gi in range(len(POOL_WINDOWS))],
        axis=-1) * ps_ref[...]
    merged = (jax.nn.sigmoid(ga_ref[0]) * _bdot(ao_ref[0], wba_ref[...])
              + jax.nn.sigmoid(gb_ref[0]) * _bdot(po, wbp_ref[...]))
    x1 = x_ref[0] + g1_ref[0] * _bdot(merged, wo_ref[...])
    x1_ref[0] = x1
    h2_ref[0] = _rms(x1, n2_ref[...]) * (1.0 + sc2_ref[0]) + sh2_ref[0]


def _mix(x, ao, d, ga, gb, g1, sc2, sh2, n2g, wp_bf, pscale, wba_bf, wbp_bf, wo_bf):
    bsz, s, dm = x.shape
    tm = _tile(s)
    tok = lambda w: pl.BlockSpec((1, tm, w), lambda b, j: (b, j, 0))
    full = lambda a: pl.BlockSpec(a.shape, lambda b, j: (0,) * a.ndim)
    n2g = n2g.reshape(1, dm)
    pscale = pscale.reshape(1, POOL_W)
    return pl.pallas_call(
        _mix_kernel, grid=(bsz, s // tm),
        in_specs=[tok(dm), tok(ATT_W), tok(POOL_W), tok(dm), tok(dm),
                  _mod_spec(g1, tm), _mod_spec(sc2, tm), _mod_spec(sh2, tm), full(n2g),
                  full(wp_bf), full(pscale), full(wba_bf), full(wbp_bf), full(wo_bf)],
        out_specs=[tok(dm), tok(dm)],
        out_shape=[jax.ShapeDtypeStruct((bsz, s, dm), F32)] * 2,
        compiler_params=_params(2), name="mix")(
            x, ao, d, ga, gb, g1, sc2, sh2, n2g, wp_bf, pscale, wba_bf, wbp_bf, wo_bf)


def _route_kernel(h2_ref, wq_ref, sk_ref, eid_ref, gate_ref, sc_ref):
    tm = h2_ref.shape[0]
    hb = h2_ref[...].astype(BF16)
    for hc in range(2 * PEER_HEADS):
        q = jnp.dot(hb, wq_ref[:, hc * LANES:(hc + 1) * LANES], preferred_element_type=F32)
        sc_ref[hc] = _bdot_nt(q, sk_ref[hc])
    lane = lax.broadcasted_iota(jnp.int32, (tm, LANES), 1)
    lane_f = lane.astype(F32)
    lane_hi = lax.shift_right_logical(lane, 4)
    lane_lo = lane & (PEER_TOPK - 1)
    zeros = jnp.zeros((tm, LANES), F32)
    half = LANES // PEER_TOPK

    def take_max(s):
        m = jnp.max(s, axis=-1, keepdims=True)
        idx = jnp.min(jnp.where(s == m, lane_f, float(LANES)), axis=-1, keepdims=True)
        return m, idx, jnp.where(lane_f == idx, NEG_INF, s)

    def head_body(h, _):
        def first_half(i, c):
            s, v_lo, v_hi, i_lo, i_hi = c
            m, idx, s = take_max(s)
            at_lo = lane_hi == i
            at_hi = lane_hi == i - half
            return (s, jnp.where(at_lo, m, v_lo), jnp.where(at_hi, m, v_hi),
                    jnp.where(at_lo, idx, i_lo), jnp.where(at_hi, idx, i_hi))

        def second_half(j, c):
            s, v, ix = c
            m, idx, s = take_max(s)
            at = lane_lo == j
            return s, jnp.where(at, m, v), jnp.where(at, idx, ix)

        _, a_lo, a_hi, ia_lo, ia_hi = lax.fori_loop(
            0, PEER_TOPK, first_half, (sc_ref[2 * h], zeros, zeros, zeros, zeros))
        _, b_t, ib_t = lax.fori_loop(0, PEER_TOPK, second_half, (sc_ref[2 * h + 1], zeros, zeros))
        e_lo = ia_lo * float(PEER_NKEYS) + ib_t
        e_hi = ia_hi * float(PEER_NKEYS) + ib_t

        def pick(k, c):
            c_lo, c_hi, best, eids = c
            m = jnp.maximum(jnp.max(c_lo, axis=-1, keepdims=True), jnp.max(c_hi, axis=-1, keepdims=True))
            p_lo = jnp.min(jnp.where(c_lo == m, lane_f, 2.0 * LANES), axis=-1, keepdims=True)
            p_hi = jnp.min(jnp.where(c_hi == m, lane_f + LANES, 2.0 * LANES), axis=-1, keepdims=True)
            p = jnp.minimum(p_lo, p_hi)
            sel_lo = lane_f == p
            sel_hi = lane_f + LANES == p
            e = jnp.maximum(jnp.max(jnp.where(sel_lo, e_lo, -1.0), axis=-1, keepdims=True),
                            jnp.max(jnp.where(sel_hi, e_hi, -1.0), axis=-1, keepdims=True))
            at = lane == k
            return (jnp.where(sel_lo, NEG_INF, c_lo), jnp.where(sel_hi, NEG_INF, c_hi),
                    jnp.where(at, m, best), jnp.where(at, e, eids))

        _, _, best, eids = lax.fori_loop(0, PEER_TOPK, pick, (a_lo + b_t, a_hi + b_t, zeros, zeros))
        live = lane < PEER_TOPK
        mx = jnp.max(jnp.where(live, best, NEG_INF), axis=-1, keepdims=True)
        ex = jnp.where(live, jnp.exp(best - mx), 0.0)
        gate_ref[h] = ex / jnp.sum(ex, axis=-1, keepdims=True)
        eid_ref[h] = eids.astype(jnp.int32)
        return 0

    lax.fori_loop(0, PEER_HEADS, head_body, 0)


def _route(h2, wq_bf, sk_bf):
    t, dm = h2.shape
    tm = min(128, t)
    out = pl.BlockSpec((PEER_HEADS, tm, LANES), lambda i: (0, i, 0))
    eid, gate = pl.pallas_call(
        _route_kernel, grid=(t // tm,),
        in_specs=[pl.BlockSpec((tm, dm), lambda i: (i, 0)),
                  pl.BlockSpec(wq_bf.shape, lambda i: (0, 0)),
                  pl.BlockSpec(sk_bf.shape, lambda i: (0, 0, 0))],
        out_specs=[out, out],
        out_shape=[jax.ShapeDtypeStruct((PEER_HEADS, t, LANES), jnp.int32),
                   jax.ShapeDtypeStruct((PEER_HEADS, t, LANES), F32)],
        scratch_shapes=[pltpu.VMEM((2 * PEER_HEADS, tm, LANES), F32)],
        compiler_params=_params(1), name="route")(h2, wq_bf, sk_bf)
    flat = lambda a: jnp.transpose(a[:, :, :PEER_TOPK], (1, 0, 2)).reshape(t, PEER_SLOTS)
    return flat(eid), flat(gate)


def _gelu_tanh(x):
    return x * (0.5 * (1.0 + jnp.tanh(0.7978845608028654 * (x + 0.044715 * (x * x * x)))))


def _peer_kernel(eid_ref, h2_ref, gate_ref, x1_ref, g2_ref, u_hbm, v_hbm, o_ref, ubuf, vbuf, sem):
    n_tok = h2_ref.shape[0]

    def row_copies(t, k):
        e = eid_ref[t, k]
        r = t * PEER_SLOTS + k
        return (pltpu.make_async_copy(u_hbm.at[pl.ds(e, 1)], ubuf.at[pl.ds(r, 1)], sem.at[0]),
                pltpu.make_async_copy(v_hbm.at[pl.ds(e, 1)], vbuf.at[pl.ds(r, 1)], sem.at[1]))

    def start_row(r, _):
        cu, cv = row_copies(lax.shift_right_logical(r, 7), r & (PEER_SLOTS - 1))
        cu.start()
        cv.start()
        return 0

    def wait_row(r, _):
        cu, cv = row_copies(lax.shift_right_logical(r, 7), r & (PEER_SLOTS - 1))
        cu.wait()
        cv.wait()
        return 0

    lax.fori_loop(0, n_tok * PEER_SLOTS, start_row, 0)
    lax.fori_loop(0, n_tok * PEER_SLOTS, wait_row, 0)

    hb = h2_ref[...].astype(BF16)
    gate = gate_ref[...]
    row = lax.broadcasted_iota(jnp.int32, (n_tok, 1), 0)
    acc = jnp.zeros(o_ref.shape, F32)
    for t in range(n_tok):
        rows = pl.ds(t * PEER_SLOTS, PEER_SLOTS)
        act = _gelu_tanh(_bdot_nt(hb, ubuf[rows, :]))
        out_t = _bdot(gate * act, vbuf[rows, :])
        acc = jnp.where(row == t, out_t, acc)
    o_ref[...] = x1_ref[...] + g2_ref[:, 0, :] * acc


def _peer(eid, h2, gate, x1, g2, u_tab, v_tab, rows_per_mod):
    t, dm = h2.shape
    nt = PEER_TOKENS
    if rows_per_mod == 1:
        g2_spec = pl.BlockSpec((nt, 1, dm), lambda i: (i, 0, 0))
    else:
        assert rows_per_mod % nt == 0
        g2_spec = pl.BlockSpec((1, 1, dm), lambda i: ((i * nt) // rows_per_mod, 0, 0))
    tok = lambda w: pl.BlockSpec((nt, w), lambda i: (i, 0))
    return pl.pallas_call(
        _peer_kernel, grid=(t // nt,),
        in_specs=[pl.BlockSpec((nt, PEER_SLOTS), lambda i: (i, 0), memory_space=pltpu.SMEM),
                  tok(dm), tok(PEER_SLOTS), tok(dm), g2_spec,
                  pl.BlockSpec(memory_space=pl.ANY), pl.BlockSpec(memory_space=pl.ANY)],
        out_specs=tok(dm),
        out_shape=jax.ShapeDtypeStruct((t, dm), F32),
        scratch_shapes=[pltpu.VMEM((nt * PEER_SLOTS, dm), F32),
                        pltpu.VMEM((nt * PEER_SLOTS, dm), F32),
                        pltpu.SemaphoreType.DMA((2,))],
        compiler_params=_params(1), name="peer")(eid, h2, gate, x1, g2, u_tab, v_tab)


def _route_t_kernel(h2_ref, wq_ref, sk_ref, eid_ref, gate_ref, sc_ref):
    tm = h2_ref.shape[0]
    hb = h2_ref[...].astype(BF16)
    for hc in range(2 * PEER_HEADS):
        q = jnp.dot(hb, wq_ref[:, hc * LANES:(hc + 1) * LANES], preferred_element_type=F32)
        sc_ref[hc] = _bdot_nt(sk_ref[hc], q)
    key_f = lax.broadcasted_iota(jnp.int32, (PEER_NKEYS, tm), 0).astype(F32)
    n_cand = PEER_TOPK * PEER_TOPK
    cand_f = lax.broadcasted_iota(jnp.int32, (n_cand, tm), 0).astype(F32)
    rank = lax.broadcasted_iota(jnp.int32, (PEER_TOPK, tm), 0)
    zeros = jnp.zeros((PEER_TOPK, tm), F32)

    def top_half(s):
        def body(i, c):
            s, v, ix = c
            m = jnp.max(s, axis=0, keepdims=True)
            idx = jnp.min(jnp.where(s == m, key_f, float(PEER_NKEYS)), axis=0, keepdims=True)
            at = rank == i
            return jnp.where(key_f == idx, NEG_INF, s), jnp.where(at, m, v), jnp.where(at, idx, ix)
        _, v, ix = lax.fori_loop(0, PEER_TOPK, body, (s, zeros, zeros))
        return v, ix

    def head_body(h, _):
        a, ia = top_half(sc_ref[2 * h])
        b, ib = top_half(sc_ref[2 * h + 1])
        cand = jnp.concatenate([a[i:i + 1] + b for i in range(PEER_TOPK)], axis=0)
        expert = jnp.concatenate([ia[i:i + 1] * float(PEER_NKEYS) + ib for i in range(PEER_TOPK)], axis=0)

        def pick(k, c):
            cd, best, eids = c
            m = jnp.max(cd, axis=0, keepdims=True)
            p = jnp.min(jnp.where(cd == m, cand_f, float(n_cand)), axis=0, keepdims=True)
            sel = cand_f == p
            e = jnp.max(jnp.where(sel, expert, -1.0), axis=0, keepdims=True)
            at = rank == k
            return jnp.where(sel, NEG_INF, cd), jnp.where(at, m, best), jnp.where(at, e, eids)

        _, best, eids = lax.fori_loop(0, PEER_TOPK, pick, (cand, zeros, zeros))
        ex = jnp.exp(best - jnp.max(best, axis=0, keepdims=True))
        gate_ref[h] = ex / jnp.sum(ex, axis=0, keepdims=True)
        eid_ref[h] = eids.astype(jnp.int32)
        return 0

    lax.fori_loop(0, PEER_HEADS, head_body, 0)


def _route_t(h2, wq_bf, sk_bf):
    t, dm = h2.shape
    tm = min(LANES, t)
    out = pl.BlockSpec((PEER_HEADS, PEER_TOPK, tm), lambda i: (0, 0, i))
    eid, gate = pl.pallas_call(
        _route_t_kernel, grid=(t // tm,),
        in_specs=[pl.BlockSpec((tm, dm), lambda i: (i, 0)),
                  pl.BlockSpec(wq_bf.shape, lambda i: (0, 0)),
                  pl.BlockSpec(sk_bf.shape, lambda i: (0, 0, 0))],
        out_specs=[out, out],
        out_shape=[jax.ShapeDtypeStruct((PEER_HEADS, PEER_TOPK, t), jnp.int32),
                   jax.ShapeDtypeStruct((PEER_HEADS, PEER_TOPK, t), F32)],
        scratch_shapes=[pltpu.VMEM((2 * PEER_HEADS, PEER_NKEYS, tm), F32)],
        compiler_params=_params(1), name="route")(h2, wq_bf, sk_bf)
    return eid.reshape(PEER_SLOTS, t).T, gate.reshape(PEER_SLOTS, t).T


PEER_STEP_TOKENS = 16
PEER_GROUP = 8


def _peer_db_kernel(eid0_ref, eidn_ref, h2_ref, gate_ref, x1_ref, g2_ref, uv_hbm, o_ref, buf, sem):
    i = pl.program_id(0)
    n_tok, dm = h2_ref.shape
    n_rows = n_tok * PEER_SLOTS
    slot = i & 1

    def issue(eid_ref, dst_slot):
        def body(r2, _):
            for par in range(2):
                r = 2 * r2 + par
                e = eid_ref[lax.shift_right_logical(r, 7), r & (PEER_SLOTS - 1)]
                pltpu.make_async_copy(uv_hbm.at[pl.ds(e, 1)], buf.at[dst_slot, pl.ds(r, 1)],
                                      sem.at[dst_slot]).start(priority=par)
            return 0
        lax.fori_loop(0, n_rows // 2, body, 0, unroll=4)

    @pl.when(i == 0)
    def _():
        issue(eid0_ref, 0)

    @pl.when(i + 1 < pl.num_programs(0))
    def _():
        issue(eidn_ref, 1 - slot)

    pltpu.make_async_copy(uv_hbm.at[pl.ds(0, n_rows)], buf.at[slot], sem.at[slot]).wait()

    row = lax.broadcasted_iota(jnp.int32, (PEER_GROUP, 1), 0)
    g2 = g2_ref[:, 0, :]
    for g in range(n_tok // PEER_GROUP):
        gs = slice(g * PEER_GROUP, (g + 1) * PEER_GROUP)
        hb = h2_ref[gs, :].astype(BF16)
        gate = gate_ref[gs, :]
        acc = jnp.zeros((PEER_GROUP, dm), F32)
        for tt in range(PEER_GROUP):
            rows = pl.ds((g * PEER_GROUP + tt) * PEER_SLOTS, PEER_SLOTS)
            w = buf[slot, rows, :]
            u_t = lax.bitcast_convert_type(w << 16, F32).astype(BF16)
            v_t = lax.bitcast_convert_type(w & jnp.uint32(0xFFFF0000), F32).astype(BF16)
            act = _gelu_tanh(lax.dot_general(hb, u_t, NT_DIMS, preferred_element_type=F32))
            out_t = jnp.dot((gate * act).astype(BF16), v_t, preferred_element_type=F32)
            acc = jnp.where(row == tt, out_t, acc)
        o_ref[gs, :] = x1_ref[gs, :] + (g2 if g2.shape[0] == 1 else g2[gs, :]) * acc


def _peer_db(eid, h2, gate, x1, g2, uv_bf, rows_per_mod):
    t, dm = h2.shape
    nt = PEER_STEP_TOKENS
    n_steps = t // nt
    assert t % nt == 0 and uv_bf.shape[1] == dm and uv_bf.dtype == jnp.uint32
    if rows_per_mod == 1:
        g2_spec = pl.BlockSpec((nt, 1, dm), lambda i: (i, 0, 0))
    else:
        assert rows_per_mod % nt == 0
        g2_spec = pl.BlockSpec((1, 1, dm), lambda i: ((i * nt) // rows_per_mod, 0, 0))
    tok = lambda w: pl.BlockSpec((nt, w), lambda i: (i, 0))
    return pl.pallas_call(
        _peer_db_kernel, grid=(n_steps,),
        in_specs=[pl.BlockSpec((nt, PEER_SLOTS), lambda i: (i, 0), memory_space=pltpu.SMEM),
                  pl.BlockSpec((nt, PEER_SLOTS), lambda i: (jnp.minimum(i + 1, n_steps - 1), 0),
                               memory_space=pltpu.SMEM),
                  tok(dm), tok(PEER_SLOTS), tok(dm), g2_spec, pl.BlockSpec(memory_space=pl.ANY)],
        out_specs=tok(dm),
        out_shape=jax.ShapeDtypeStruct((t, dm), F32),
        scratch_shapes=[pltpu.VMEM((2, nt * PEER_SLOTS, dm), jnp.uint32),
                        pltpu.SemaphoreType.DMA((2,))],
        compiler_params=_params(1), name="peer")(eid, eid, h2, gate, x1, g2, uv_bf)


def _pack_bf16_pair(lo, hi):
    bits = lambda a: lax.bitcast_convert_type(a.astype(BF16), jnp.uint16).astype(jnp.uint32)
    return bits(lo) | (bits(hi) << 16)


def _channel_mix(x, ao, d, ga, gb, mods, wts):
    g1, sh2, sc2, g2 = mods
    bsz, s, dm = x.shape
    x1, h2 = _mix(x, ao, d, ga, gb, g1, sc2, sh2, wts["norm2_g"], wts["w_pool"], wts["pool_scale"],
                  wts["w_branch_attn"], wts["w_branch_pool"], wts["w_out"])
    h2f = h2.reshape(bsz * s, dm)
    eid, gate = _route_t(h2f, wts["peer_wq"], wts["peer_subkeys"])
    per_token = g2.shape[1] != 1
    g2f = g2.reshape(-1, 1, dm)
    out = _peer_db(eid, h2f, gate, x1.reshape(bsz * s, dm), g2f, wts["peer_uv"], 1 if per_token else s)
    return out.reshape(bsz, s, dm)


def kernel(x_prompt, x_sample, cache_k, cache_v, state_pool, c_prompt, c_sample, norm1_g, norm2_g, w_ada, b_ada, w_in, q_norm_g, k_norm_g, attn_sinks, w_pool, pool_scale, w_branch_attn, w_branch_pool, w_out, peer_wq, peer_subkeys, peer_u, peer_v):
    depth = w_ada.shape[0]
    bsz, seq, dm = x_prompt.shape
    nd = x_sample.shape[0]
    assert x_sample.shape[1] == 1
    yp = x_prompt
    ys = x_sample.reshape(1, nd, dm)
    kp, vp, pp, ksm, vsm, psm = [], [], [], [], [], []
    for l in range(depth):
        wts = {
            "norm2_g": norm2_g[l], "pool_scale": pool_scale[l],
            "w_pool": w_pool[l].astype(BF16), "w_branch_attn": w_branch_attn[l].astype(BF16),
            "w_branch_pool": w_branch_pool[l].astype(BF16), "w_out": w_out[l].astype(BF16),
            "peer_wq": peer_wq[l].astype(BF16),
            "peer_subkeys": peer_subkeys[l].reshape(2 * PEER_HEADS, PEER_NKEYS, -1).astype(BF16),
            "peer_uv": _pack_bf16_pair(peer_u[l], peer_v[l]),
        }
        w_in_bf = w_in[l].astype(BF16)
        m = _ada(jnp.concatenate([c_prompt, c_sample], axis=0), w_ada[l].astype(BF16), b_ada[l])
        m6 = [m[:, i * dm:(i + 1) * dm] for i in range(6)]
        mp = [a[:bsz].reshape(bsz, 1, dm) for a in m6]
        ms = [a[bsz:].reshape(1, nd, dm) for a in m6]

        q, k, v, p, ga, gb = _in_proj(yp, mp[1], mp[0], norm1_g[l], w_in_bf)
        ao, kn = _attn_prompt(q, k, v, q_norm_g[l], k_norm_g[l], attn_sinks[l])
        kp.append(kn[:, -WINDOW:].reshape(bsz, WINDOW, N_KV_HEADS, HEAD_DIM))
        vp.append(v[:, -WINDOW:].reshape(bsz, WINDOW, N_KV_HEADS, HEAD_DIM))
        pp.append(p[:, -POOL_HIST:])
        yp = _channel_mix(yp, ao, _pool_prompt(p), ga, gb, (mp[2], mp[3], mp[4], mp[5]), wts)

        q, k, v, p, ga, gb = _in_proj(ys, ms[1], ms[0], norm1_g[l], w_in_bf)
        ck = cache_k[l]
        cv = cache_v[l]
        win = ck.shape[1]
        ao, kn = _attn_sample(q.reshape(nd, N_HEADS, HEAD_DIM), k.reshape(nd, N_KV_HEADS, HEAD_DIM),
                              v.reshape(nd, N_KV_HEADS, HEAD_DIM), ck.reshape(nd, win, KV_W),
                              cv.reshape(nd, win, KV_W), q_norm_g[l], k_norm_g[l], attn_sinks[l])
        ksm.append(jnp.concatenate([ck, kn[:, None]], axis=1)[:, -win:])
        vsm.append(jnp.concatenate([cv, v.reshape(nd, 1, N_KV_HEADS, HEAD_DIM)], axis=1)[:, -win:])
        st = state_pool[l]
        psm.append(jnp.concatenate([st, p.reshape(nd, 1, POOL_W)], axis=1)[:, -st.shape[1]:])
        d = _pool_sample(p.reshape(nd, POOL_W), st).reshape(1, nd, POOL_W)
        ys = _channel_mix(ys, ao.reshape(1, nd, ATT_W), d, ga, gb, (ms[2], ms[3], ms[4], ms[5]), wts)
    return (yp, ys.reshape(nd, 1, dm), jnp.stack(kp), jnp.stack(vp), jnp.stack(pp),
            jnp.stack(ksm), jnp.stack(vsm), jnp.stack(psm))
```

```python
import functools

import jax
import jax.numpy as jnp
from jax import lax
from jax.experimental import pallas as pl
from jax.experimental.pallas import tpu as pltpu

F32 = jnp.float32
BF16 = jnp.bfloat16
NORM_EPS = 1e-6
N_HEADS = 8
N_KV_HEADS = 2
Q_GROUP = N_HEADS // N_KV_HEADS
HEAD_DIM = 64
WINDOW = 128
ATT_W = N_HEADS * HEAD_DIM
KV_W = N_KV_HEADS * HEAD_DIM
POOL_WINDOWS = (2, 4, 8, 16)
POOL_GC = 128
POOL_W = POOL_GC * len(POOL_WINDOWS)
POOL_HIST = max(POOL_WINDOWS) - 1
HIST_ROWS = 16
PEER_HEADS = 8
PEER_NKEYS = 128
PEER_TOPK = 16
PEER_SLOTS = PEER_HEADS * PEER_TOPK
PEER_TOKENS = 8
LANES = 128
VMEM_LIMIT_BYTES = 48 * 1024 * 1024
NEG_INF = float("-inf")
NT_DIMS = (((1,), (1,)), ((), ()))


def _params(n_axes):
    return pltpu.CompilerParams(dimension_semantics=("arbitrary",) * n_axes,
                                vmem_limit_bytes=VMEM_LIMIT_BYTES)


def _rms(x, g):
    return x * lax.rsqrt(jnp.mean(x * x, axis=-1, keepdims=True) + NORM_EPS) * g


def _bdot(a, b):
    return jnp.dot(a.astype(BF16), b.astype(BF16), preferred_element_type=F32)


def _bdot_nt(a, b):
    return lax.dot_general(a.astype(BF16), b.astype(BF16), NT_DIMS, preferred_element_type=F32)


def _ada_kernel(c_ref, w_ref, b_ref, o_ref):
    c = c_ref[...]
    o_ref[...] = _bdot(c * jax.nn.sigmoid(c), w_ref[...]) + b_ref[...]


def _ada(c, w_bf, b):
    n, d = c.shape
    nout = w_bf.shape[1]
    tn = nout // 4
    return pl.pallas_call(
        _ada_kernel, grid=(nout // tn,),
        in_specs=[pl.BlockSpec((n, d), lambda j: (0, 0)),
                  pl.BlockSpec((d, tn), lambda j: (0, j)),
                  pl.BlockSpec((1, tn), lambda j: (0, j))],
        out_specs=pl.BlockSpec((n, tn), lambda j: (0, j)),
        out_shape=jax.ShapeDtypeStruct((n, nout), F32),
        compiler_params=_params(1), name="ada")(c, w_bf, b.reshape(1, nout))


_IN_CUTS = (0, ATT_W, ATT_W + KV_W, ATT_W + 2 * KV_W, ATT_W + 2 * KV_W + POOL_W)


def _in_kernel(x_ref, sc_ref, sh_ref, g_ref, w_ref, q_ref, k_ref, v_ref, p_ref, ga_ref, gb_ref):
    d = x_ref.shape[-1]
    h = _rms(x_ref[0], g_ref[...]) * (1.0 + sc_ref[0]) + sh_ref[0]
    hb = h.astype(BF16)
    cuts = _IN_CUTS + (_IN_CUTS[-1] + d, _IN_CUTS[-1] + 2 * d)
    for o_ref, lo, hi in zip((q_ref, k_ref, v_ref, p_ref, ga_ref, gb_ref), cuts[:-1], cuts[1:]):
        o_ref[0] = jnp.dot(hb, w_ref[:, lo:hi], preferred_element_type=F32)


def _mod_spec(mod, tm):
    if mod.shape[1] == 1:
        return pl.BlockSpec((1, 1, mod.shape[2]), lambda b, j: (b, 0, 0))
    return pl.BlockSpec((1, tm, mod.shape[2]), lambda b, j: (b, j, 0))


def _tile(s):
    return min(256, s)


def _in_proj(x, sc, sh, g, w_bf):
    bsz, s, d = x.shape
    tm = _tile(s)
    widths = (ATT_W, KV_W, KV_W, POOL_W, d, d)
    tok = lambda w: pl.BlockSpec((1, tm, w), lambda b, j: (b, j, 0))
    return pl.pallas_call(
        _in_kernel, grid=(bsz, s // tm),
        in_specs=[tok(d), _mod_spec(sc, tm), _mod_spec(sh, tm),
                  pl.BlockSpec((1, d), lambda b, j: (0, 0)),
                  pl.BlockSpec(w_bf.shape, lambda b, j: (0, 0))],
        out_specs=[tok(w) for w in widths],
        out_shape=[jax.ShapeDtypeStruct((bsz, s, w), F32) for w in widths],
        compiler_params=_params(2), name="in_proj")(x, sc, sh, g.reshape(1, d), w_bf)


def _attn_p_kernel(sinks_ref, q_ref, kc_ref, kp_ref, vc_ref, vp_ref, qg_ref, kg_ref, o_ref, kn_ref):
    j = pl.program_id(1)
    blk = q_ref.shape[1]
    qi = lax.broadcasted_iota(jnp.int32, (blk, 2 * blk), 0)
    kj = lax.broadcasted_iota(jnp.int32, (blk, 2 * blk), 1)
    dist = qi + blk - kj
    valid = (dist >= 0) & (dist <= WINDOW) & ((kj >= blk) | (j > 0))
    dist_f = dist.astype(F32)
    qg = qg_ref[...]
    kg = kg_ref[...]
    for kv in range(N_KV_HEADS):
        ks = slice(kv * HEAD_DIM, (kv + 1) * HEAD_DIM)
        kcn = _rms(kc_ref[0, :, ks], kg)
        kn_ref[0, :, ks] = kcn
        kk = jnp.concatenate([_rms(kp_ref[0, :, ks], kg), kcn], axis=0).astype(BF16)
        vv = jnp.concatenate([vp_ref[0, :, ks], vc_ref[0, :, ks]], axis=0).astype(BF16)
        for g in range(Q_GROUP):
            h = kv * Q_GROUP + g
            hs = slice(h * HEAD_DIM, (h + 1) * HEAD_DIM)
            qn = _rms(q_ref[0, :, hs], qg)
            s = _bdot_nt(qn, kk) * (HEAD_DIM ** -0.5)
            s = s - (2.0 ** -(h + 1)) * dist_f
            s = jnp.where(valid, s, -1e30)
            sink = sinks_ref[h]
            m = jnp.maximum(jnp.max(s, axis=-1, keepdims=True), sink)
            e = jnp.exp(s - m)
            den = jnp.sum(e, axis=-1, keepdims=True) + jnp.exp(sink - m)
            o_ref[0, :, hs] = jnp.dot((e / den).astype(BF16), vv, preferred_element_type=F32)


def _attn_prompt(q, k, v, qg, kg, sinks):
    bsz, s, _ = q.shape
    blk = WINDOW
    cur = lambda w: pl.BlockSpec((1, blk, w), lambda b, j: (b, j, 0))
    prev = lambda w: pl.BlockSpec((1, blk, w), lambda b, j: (b, jnp.maximum(j - 1, 0), 0))
    vec = pl.BlockSpec((1, HEAD_DIM), lambda b, j: (0, 0))
    return pl.pallas_call(
        _attn_p_kernel, grid=(bsz, s // blk),
        in_specs=[pl.BlockSpec(memory_space=pltpu.SMEM), cur(ATT_W), cur(KV_W), prev(KV_W),
                  cur(KV_W), prev(KV_W), vec, vec],
        out_specs=[cur(ATT_W), cur(KV_W)],
        out_shape=[jax.ShapeDtypeStruct((bsz, s, ATT_W), F32),
                   jax.ShapeDtypeStruct((bsz, s, KV_W), F32)],
        compiler_params=_params(2), name="attn_prompt")(
            sinks, q, k, k, v, v, qg.reshape(1, HEAD_DIM), kg.reshape(1, HEAD_DIM))


def _row_consts(vals):
    row = lax.broadcasted_iota(jnp.int32, (N_HEADS, 1), 0)
    out = jnp.zeros((N_HEADS, 1), F32)
    for h, v in enumerate(vals):
        out = jnp.where(row == h, v, out)
    return out


def _attn_s_kernel(sinks_ref, q_ref, kx_ref, vx_ref, ck_ref, cv_ref, qg_ref, kg_ref, o_ref, kn_ref):
    nb = q_ref.shape[0]
    w = ck_ref.shape[1]
    slope = _row_consts([2.0 ** -(h + 1) for h in range(N_HEADS)])
    sink = _row_consts([sinks_ref[h] for h in range(N_HEADS)])
    row = lax.broadcasted_iota(jnp.int32, (N_HEADS, 1), 0)
    first = row < Q_GROUP
    dist_c = (w - lax.broadcasted_iota(jnp.int32, (1, w), 1)).astype(F32)
    scale = HEAD_DIM ** -0.5
    for b in range(nb):
        qn = _rms(q_ref[b], qg_ref[...])
        kxn = _rms(kx_ref[b], kg_ref[...])
        kn_ref[b] = kxn
        vx = vx_ref[b]
        ck = ck_ref[b]
        cv = cv_ref[b]
        kx_h = jnp.where(first, kxn[0:1], kxn[1:2])
        vx_h = jnp.where(first, vx[0:1], vx[1:2])
        s_c = jnp.where(first, _bdot_nt(qn, ck[:, :HEAD_DIM]), _bdot_nt(qn, ck[:, HEAD_DIM:]))
        s_c = s_c * scale - slope * dist_c
        s_n = jnp.sum(qn * kx_h, axis=-1, keepdims=True) * scale
        m = jnp.maximum(jnp.maximum(jnp.max(s_c, axis=-1, keepdims=True), s_n), sink)
        e_c = jnp.exp(s_c - m)
        e_n = jnp.exp(s_n - m)
        den = jnp.sum(e_c, axis=-1, keepdims=True) + e_n + jnp.exp(sink - m)
        p_c = e_c / den
        o_c = jnp.where(first, _bdot(p_c, cv[:, :HEAD_DIM]), _bdot(p_c, cv[:, HEAD_DIM:]))
        o_ref[b] = o_c + (e_n / den) * vx_h


def _attn_sample(q, kx, vx, ck, cv, qg, kg, sinks):
    n = q.shape[0]
    w = ck.shape[1]
    nb = 8
    blk = lambda a, c: pl.BlockSpec((nb, a, c), lambda i: (i, 0, 0))
    vec = pl.BlockSpec((1, HEAD_DIM), lambda i: (0, 0))
    return pl.pallas_call(
        _attn_s_kernel, grid=(n // nb,),
        in_specs=[pl.BlockSpec(memory_space=pltpu.SMEM), blk(N_HEADS, HEAD_DIM),
                  blk(N_KV_HEADS, HEAD_DIM), blk(N_KV_HEADS, HEAD_DIM), blk(w, KV_W), blk(w, KV_W),
                  vec, vec],
        out_specs=[blk(N_HEADS, HEAD_DIM), blk(N_KV_HEADS, HEAD_DIM)],
        out_shape=[jax.ShapeDtypeStruct((n, N_HEADS, HEAD_DIM), F32),
                   jax.ShapeDtypeStruct((n, N_KV_HEADS, HEAD_DIM), F32)],
        compiler_params=_params(1), name="attn_sample")(
            sinks, q, kx, vx, ck, cv, qg.reshape(1, HEAD_DIM), kg.reshape(1, HEAD_DIM))


def _pool_p_kernel(p_ref, hist_ref, d_ref, ext_ref):
    j = pl.program_id(1)
    tm = p_ref.shape[1]
    cur = p_ref[0]
    ext_ref[0:HIST_ROWS, :] = hist_ref[0, 0]
    ext_ref[HIST_ROWS:, :] = cur
    pos = j * tm + lax.broadcasted_iota(jnp.int32, (tm, 1), 0)
    acc = cur
    lo = 1
    for gi, w in enumerate(POOL_WINDOWS):
        for s in range(lo, w):
            acc = acc + ext_ref[HIST_ROWS - s:HIST_ROWS - s + tm, :]
        lo = w
        gs = slice(gi * POOL_GC, (gi + 1) * POOL_GC)
        cnt = jnp.minimum(pos + 1, w).astype(F32)
        d_ref[0, :, gs] = acc[:, gs] / cnt - cur[:, gs]


def _pool_prompt(p):
    bsz, s, pw = p.shape
    tm = _tile(s)
    nt = s // tm
    tails = p.reshape(bsz, nt, tm, pw)[:, :, tm - HIST_ROWS:, :]
    hist = jnp.concatenate([jnp.zeros((bsz, 1, HIST_ROWS, pw), F32), tails[:, :-1]], axis=1)
    return pl.pallas_call(
        _pool_p_kernel, grid=(bsz, nt),
        in_specs=[pl.BlockSpec((1, tm, pw), lambda b, j: (b, j, 0)),
                  pl.BlockSpec((1, 1, HIST_ROWS, pw), lambda b, j: (b, j, 0, 0))],
        out_specs=pl.BlockSpec((1, tm, pw), lambda b, j: (b, j, 0)),
        out_shape=jax.ShapeDtypeStruct((bsz, s, pw), F32),
        scratch_shapes=[pltpu.VMEM((tm + HIST_ROWS, pw), F32)],
        compiler_params=_params(2), name="pool_prompt")(p, hist)


def _pool_s_kernel(p_ref, st_ref, d_ref):
    n_hist = st_ref.shape[1]
    cur = p_ref[...]
    acc = cur
    lo = 1
    for gi, w in enumerate(POOL_WINDOWS):
        for s in range(lo, w):
            acc = acc + st_ref[:, n_hist - s, :]
        lo = w
        gs = slice(gi * POOL_GC, (gi + 1) * POOL_GC)
        d_ref[:, gs] = acc[:, gs] / float(w) - cur[:, gs]


def _pool_sample(p, state):
    assert state.shape[1] == POOL_HIST
    return pl.pallas_call(
        _pool_s_kernel, out_shape=jax.ShapeDtypeStruct(p.shape, F32),
        compiler_params=pltpu.CompilerParams(vmem_limit_bytes=VMEM_LIMIT_BYTES),
        name="pool_sample")(p, state)


def _mix_kernel(x_ref, ao_ref, d_ref, ga_ref, gb_ref, g1_ref, sc2_ref, sh2_ref, n2_ref,
                wp_ref, ps_ref, wba_ref, wbp_ref, wo_ref, x1_ref, h2_ref):
    d = d_ref[0]
    po = jnp.concatenate(
        [_bdot(d[:, gi * POOL_GC:(gi + 1) * POOL_GC], wp_ref[gi]) for gi in range(len(POOL_WINDOWS))],
        axis=-1) * ps_ref[...]
    merged = (jax.nn.sigmoid(ga_ref[0]) * _bdot(ao_ref[0], wba_ref[...])
              + jax.nn.sigmoid(gb_ref[0]) * _bdot(po, wbp_ref[...]))
    x1 = x_ref[0] + g1_ref[0] * _bdot(merged, wo_ref[...])
    x1_ref[0] = x1
    h2_ref[0] = _rms(x1, n2_ref[...]) * (1.0 + sc2_ref[0]) + sh2_ref[0]


def _mix(x, ao, d, ga, gb, g1, sc2, sh2, n2g, wp_bf, pscale, wba_bf, wbp_bf, wo_bf):
    bsz, s, dm = x.shape
    tm = _tile(s)
    tok = lambda w: pl.BlockSpec((1, tm, w), lambda b, j: (b, j, 0))
    full = lambda a: pl.BlockSpec(a.shape, lambda b, j: (0,) * a.ndim)
    n2g = n2g.reshape(1, dm)
    pscale = pscale.reshape(1, POOL_W)
    return pl.pallas_call(
        _mix_kernel, grid=(bsz, s // tm),
        in_specs=[tok(dm), tok(ATT_W), tok(POOL_W), tok(dm), tok(dm),
                  _mod_spec(g1, tm), _mod_spec(sc2, tm), _mod_spec(sh2, tm), full(n2g),
                  full(wp_bf), full(pscale), full(wba_bf), full(wbp_bf), full(wo_bf)],
        out_specs=[tok(dm), tok(dm)],
        out_shape=[jax.ShapeDtypeStruct((bsz, s, dm), F32)] * 2,
        compiler_params=_params(2), name="mix")(
            x, ao, d, ga, gb, g1, sc2, sh2, n2g, wp_bf, pscale, wba_bf, wbp_bf, wo_bf)


def _route_kernel(h2_ref, wq_ref, sk_ref, eid_ref, gate_ref, sc_ref):
    tm = h2_ref.shape[0]
    hb = h2_ref[...].astype(BF16)
    for hc in range(2 * PEER_HEADS):
        q = jnp.dot(hb, wq_ref[:, hc * LANES:(hc + 1) * LANES], preferred_element_type=F32)
        sc_ref[hc] = _bdot_nt(q, sk_ref[hc])
    lane = lax.broadcasted_iota(jnp.int32, (tm, LANES), 1)
    lane_f = lane.astype(F32)
    lane_hi = lax.shift_right_logical(lane, 4)
    lane_lo = lane & (PEER_TOPK - 1)
    zeros = jnp.zeros((tm, LANES), F32)
    half = LANES // PEER_TOPK

    def take_max(s):
        m = jnp.max(s, axis=-1, keepdims=True)
        idx = jnp.min(jnp.where(s == m, lane_f, float(LANES)), axis=-1, keepdims=True)
        return m, idx, jnp.where(lane_f == idx, NEG_INF, s)

    def head_body(h, _):
        def first_half(i, c):
            s, v_lo, v_hi, i_lo, i_hi = c
            m, idx, s = take_max(s)
            at_lo = lane_hi == i
            at_hi = lane_hi == i - half
            return (s, jnp.where(at_lo, m, v_lo), jnp.where(at_hi, m, v_hi),
                    jnp.where(at_lo, idx, i_lo), jnp.where(at_hi, idx, i_hi))

        def second_half(j, c):
            s, v, ix = c
            m, idx, s = take_max(s)
            at = lane_lo == j
            return s, jnp.where(at, m, v), jnp.where(at, idx, ix)

        _, a_lo, a_hi, ia_lo, ia_hi = lax.fori_loop(
            0, PEER_TOPK, first_half, (sc_ref[2 * h], zeros, zeros, zeros, zeros))
        _, b_t, ib_t = lax.fori_loop(0, PEER_TOPK, second_half, (sc_ref[2 * h + 1], zeros, zeros))
        e_lo = ia_lo * float(PEER_NKEYS) + ib_t
        e_hi = ia_hi * float(PEER_NKEYS) + ib_t

        def pick(k, c):
            c_lo, c_hi, best, eids = c
            m = jnp.maximum(jnp.max(c_lo, axis=-1, keepdims=True), jnp.max(c_hi, axis=-1, keepdims=True))
            p_lo = jnp.min(jnp.where(c_lo == m, lane_f, 2.0 * LANES), axis=-1, keepdims=True)
            p_hi = jnp.min(jnp.where(c_hi == m, lane_f + LANES, 2.0 * LANES), axis=-1, keepdims=True)
            p = jnp.minimum(p_lo, p_hi)
            sel_lo = lane_f == p
            sel_hi = lane_f + LANES == p
            e = jnp.maximum(jnp.max(jnp.where(sel_lo, e_lo, -1.0), axis=-1, keepdims=True),
                            jnp.max(jnp.where(sel_hi, e_hi, -1.0), axis=-1, keepdims=True))
            at = lane == k
            return (jnp.where(sel_lo, NEG_INF, c_lo), jnp.where(sel_hi, NEG_INF, c_hi),
                    jnp.where(at, m, best), jnp.where(at, e, eids))

        _, _, best, eids = lax.fori_loop(0, PEER_TOPK, pick, (a_lo + b_t, a_hi + b_t, zeros, zeros))
        live = lane < PEER_TOPK
        mx = jnp.max(jnp.where(live, best, NEG_INF), axis=-1, keepdims=True)
        ex = jnp.where(live, jnp.exp(best - mx), 0.0)
        gate_ref[h] = ex / jnp.sum(ex, axis=-1, keepdims=True)
        eid_ref[h] = eids.astype(jnp.int32)
        return 0

    lax.fori_loop(0, PEER_HEADS, head_body, 0)


def _route(h2, wq_bf, sk_bf):
    t, dm = h2.shape
    tm = min(128, t)
    out = pl.BlockSpec((PEER_HEADS, tm, LANES), lambda i: (0, i, 0))
    eid, gate = pl.pallas_call(
        _route_kernel, grid=(t // tm,),
        in_specs=[pl.BlockSpec((tm, dm), lambda i: (i, 0)),
                  pl.BlockSpec(wq_bf.shape, lambda i: (0, 0)),
                  pl.BlockSpec(sk_bf.shape, lambda i: (0, 0, 0))],
        out_specs=[out, out],
        out_shape=[jax.ShapeDtypeStruct((PEER_HEADS, t, LANES), jnp.int32),
                   jax.ShapeDtypeStruct((PEER_HEADS, t, LANES), F32)],
        scratch_shapes=[pltpu.VMEM((2 * PEER_HEADS, tm, LANES), F32)],
        compiler_params=_params(1), name="route")(h2, wq_bf, sk_bf)
    flat = lambda a: jnp.transpose(a[:, :, :PEER_TOPK], (1, 0, 2)).reshape(t, PEER_SLOTS)
    return flat(eid), flat(gate)


def _gelu_tanh(x):
    return x * (0.5 * (1.0 + jnp.tanh(0.7978845608028654 * (x + 0.044715 * (x * x * x)))))


def _peer_kernel(eid_ref, h2_ref, gate_ref, x1_ref, g2_ref, u_hbm, v_hbm, o_ref, ubuf, vbuf, sem):
    n_tok = h2_ref.shape[0]

    def row_copies(t, k):
        e = eid_ref[t, k]
        r = t * PEER_SLOTS + k
        return (pltpu.make_async_copy(u_hbm.at[pl.ds(e, 1)], ubuf.at[pl.ds(r, 1)], sem.at[0]),
                pltpu.make_async_copy(v_hbm.at[pl.ds(e, 1)], vbuf.at[pl.ds(r, 1)], sem.at[1]))

    def start_row(r, _):
        cu, cv = row_copies(lax.shift_right_logical(r, 7), r & (PEER_SLOTS - 1))
        cu.start()
        cv.start()
        return 0

    def wait_row(r, _):
        cu, cv = row_copies(lax.shift_right_logical(r, 7), r & (PEER_SLOTS - 1))
        cu.wait()
        cv.wait()
        return 0

    lax.fori_loop(0, n_tok * PEER_SLOTS, start_row, 0)
    lax.fori_loop(0, n_tok * PEER_SLOTS, wait_row, 0)

    hb = h2_ref[...].astype(BF16)
    gate = gate_ref[...]
    row = lax.broadcasted_iota(jnp.int32, (n_tok, 1), 0)
    acc = jnp.zeros(o_ref.shape, F32)
    for t in range(n_tok):
        rows = pl.ds(t * PEER_SLOTS, PEER_SLOTS)
        act = _gelu_tanh(_bdot_nt(hb, ubuf[rows, :]))
        out_t = _bdot(gate * act, vbuf[rows, :])
        acc = jnp.where(row == t, out_t, acc)
    o_ref[...] = x1_ref[...] + g2_ref[:, 0, :] * acc


def _peer(eid, h2, gate, x1, g2, u_tab, v_tab, rows_per_mod):
    t, dm = h2.shape
    nt = PEER_TOKENS
    if rows_per_mod == 1:
        g2_spec = pl.BlockSpec((nt, 1, dm), lambda i: (i, 0, 0))
    else:
        assert rows_per_mod % nt == 0
        g2_spec = pl.BlockSpec((1, 1, dm), lambda i: ((i * nt) // rows_per_mod, 0, 0))
    tok = lambda w: pl.BlockSpec((nt, w), lambda i: (i, 0))
    return pl.pallas_call(
        _peer_kernel, grid=(t // nt,),
        in_specs=[pl.BlockSpec((nt, PEER_SLOTS), lambda i: (i, 0), memory_space=pltpu.SMEM),
                  tok(dm), tok(PEER_SLOTS), tok(dm), g2_spec,
                  pl.BlockSpec(memory_space=pl.ANY), pl.BlockSpec(memory_space=pl.ANY)],
        out_specs=tok(dm),
        out_shape=jax.ShapeDtypeStruct((t, dm), F32),
        scratch_shapes=[pltpu.VMEM((nt * PEER_SLOTS, dm), F32),
                        pltpu.VMEM((nt * PEER_SLOTS, dm), F32),
                        pltpu.SemaphoreType.DMA((2,))],
        compiler_params=_params(1), name="peer")(eid, h2, gate, x1, g2, u_tab, v_tab)


def _route_t_kernel(h2_ref, wq_ref, sk_ref, eid_ref, gate_ref, sc_ref):
    tm = h2_ref.shape[0]
    hb = h2_ref[...].astype(BF16)
    for hc in range(2 * PEER_HEADS):
        q = jnp.dot(hb, wq_ref[:, hc * LANES:(hc + 1) * LANES], preferred_element_type=F32)
        sc_ref[hc] = _bdot_nt(sk_ref[hc], q)
    key_f = lax.broadcasted_iota(jnp.int32, (PEER_NKEYS, tm), 0).astype(F32)
    n_cand = PEER_TOPK * PEER_TOPK
    cand_f = lax.broadcasted_iota(jnp.int32, (n_cand, tm), 0).astype(F32)
    rank = lax.broadcasted_iota(jnp.int32, (PEER_TOPK, tm), 0)
    zeros = jnp.zeros((PEER_TOPK, tm), F32)

    def top_half(s):
        def body(i, c):
            s, v, ix = c
            m = jnp.max(s, axis=0, keepdims=True)
            idx = jnp.min(jnp.where(s == m, key_f, float(PEER_NKEYS)), axis=0, keepdims=True)
            at = rank == i
            return jnp.where(key_f == idx, NEG_INF, s), jnp.where(at, m, v), jnp.where(at, idx, ix)
        _, v, ix = lax.fori_loop(0, PEER_TOPK, body, (s, zeros, zeros))
        return v, ix

    def head_body(h, _):
        a, ia = top_half(sc_ref[2 * h])
        b, ib = top_half(sc_ref[2 * h + 1])
        cand = jnp.concatenate([a[i:i + 1] + b for i in range(PEER_TOPK)], axis=0)
        expert = jnp.concatenate([ia[i:i + 1] * float(PEER_NKEYS) + ib for i in range(PEER_TOPK)], axis=0)

        def pick(k, c):
            cd, best, eids = c
            m = jnp.max(cd, axis=0, keepdims=True)
            p = jnp.min(jnp.where(cd == m, cand_f, float(n_cand)), axis=0, keepdims=True)
            sel = cand_f == p
            e = jnp.max(jnp.where(sel, expert, -1.0), axis=0, keepdims=True)
            at = rank == k
            return jnp.where(sel, NEG_INF, cd), jnp.where(at, m, best), jnp.where(at, e, eids)

        _, best, eids = lax.fori_loop(0, PEER_TOPK, pick, (cand, zeros, zeros))
        ex = jnp.exp(best - jnp.max(best, axis=0, keepdims=True))
        gate_ref[h] = ex / jnp.sum(ex, axis=0, keepdims=True)
        eid_ref[h] = eids.astype(jnp.int32)
        return 0

    lax.fori_loop(0, PEER_HEADS, head_body, 0)


def _route_t(h2, wq_bf, sk_bf):
    t, dm = h2.shape
    tm = min(LANES, t)
    out = pl.BlockSpec((PEER_HEADS, PEER_TOPK, tm), lambda i: (0, 0, i))
    eid, gate = pl.pallas_call(
        _route_t_kernel, grid=(t // tm,),
        in_specs=[pl.BlockSpec((tm, dm), lambda i: (i, 0)),
                  pl.BlockSpec(wq_bf.shape, lambda i: (0, 0)),
                  pl.BlockSpec(sk_bf.shape, lambda i: (0, 0, 0))],
        out_specs=[out, out],
        out_shape=[jax.ShapeDtypeStruct((PEER_HEADS, PEER_TOPK, t), jnp.int32),
                   jax.ShapeDtypeStruct((PEER_HEADS, PEER_TOPK, t), F32)],
        scratch_shapes=[pltpu.VMEM((2 * PEER_HEADS, PEER_NKEYS, tm), F32)],
        compiler_params=_params(1), name="route")(h2, wq_bf, sk_bf)
    return eid.reshape(PEER_SLOTS, t).T, gate.reshape(PEER_SLOTS, t).T


PEER_STEP_TOKENS = 16
PEER_GROUP = 8


def _peer_db_kernel(eid0_ref, eidn_ref, h2_ref, gate_ref, x1_ref, g2_ref, uv_hbm, o_ref, buf, sem):
    i = pl.program_id(0)
    n_tok, dm = h2_ref.shape
    n_rows = n_tok * PEER_SLOTS
    slot = i & 1

    def issue(eid_ref, dst_slot):
        def body(r2, _):
            for par in range(2):
                r = 2 * r2 + par
                e = eid_ref[lax.shift_right_logical(r, 7), r & (PEER_SLOTS - 1)]
                pltpu.make_async_copy(uv_hbm.at[pl.ds(e, 1)], buf.at[dst_slot, pl.ds(r, 1)],
                                      sem.at[dst_slot]).start(priority=par)
            return 0
        lax.fori_loop(0, n_rows // 2, body, 0, unroll=4)

    @pl.when(i == 0)
    def _():
        issue(eid0_ref, 0)

    @pl.when(i + 1 < pl.num_programs(0))
    def _():
        issue(eidn_ref, 1 - slot)

    pltpu.make_async_copy(uv_hbm.at[pl.ds(0, n_rows)], buf.at[slot], sem.at[slot]).wait()

    row = lax.broadcasted_iota(jnp.int32, (PEER_GROUP, 1), 0)
    g2 = g2_ref[:, 0, :]
    for g in range(n_tok // PEER_GROUP):
        gs = slice(g * PEER_GROUP, (g + 1) * PEER_GROUP)
        hb = h2_ref[gs, :].astype(BF16)
        gate = gate_ref[gs, :]
        acc = jnp.zeros((PEER_GROUP, dm), F32)
        for tt in range(PEER_GROUP):
            rows = pl.ds((g * PEER_GROUP + tt) * PEER_SLOTS, PEER_SLOTS)
            w = buf[slot, rows, :]
            u_t = lax.bitcast_convert_type(w << 16, F32).astype(BF16)
            v_t = lax.bitcast_convert_type(w & jnp.uint32(0xFFFF0000), F32).astype(BF16)
            act = _gelu_tanh(lax.dot_general(hb, u_t, NT_DIMS, preferred_element_type=F32))
            out_t = jnp.dot((gate * act).astype(BF16), v_t, preferred_element_type=F32)
            acc = jnp.where(row == tt, out_t, acc)
        o_ref[gs, :] = x1_ref[gs, :] + (g2 if g2.shape[0] == 1 else g2[gs, :]) * acc


def _peer_db(eid, h2, gate, x1, g2, uv_bf, rows_per_mod):
    t, dm = h2.shape
    nt = PEER_STEP_TOKENS
    n_steps = t // nt
    assert t % nt == 0 and uv_bf.shape[1] == dm and uv_bf.dtype == jnp.uint32
    if rows_per_mod == 1:
        g2_spec = pl.BlockSpec((nt, 1, dm), lambda i: (i, 0, 0))
    else:
        assert rows_per_mod % nt == 0
        g2_spec = pl.BlockSpec((1, 1, dm), lambda i: ((i * nt) // rows_per_mod, 0, 0))
    tok = lambda w: pl.BlockSpec((nt, w), lambda i: (i, 0))
    return pl.pallas_call(
        _peer_db_kernel, grid=(n_steps,),
        in_specs=[pl.BlockSpec((nt, PEER_SLOTS), lambda i: (i, 0), memory_space=pltpu.SMEM),
                  pl.BlockSpec((nt, PEER_SLOTS), lambda i: (jnp.minimum(i + 1, n_steps - 1), 0),
                               memory_space=pltpu.SMEM),
                  tok(dm), tok(PEER_SLOTS), tok(dm), g2_spec, pl.BlockSpec(memory_space=pl.ANY)],
        out_specs=tok(dm),
        out_shape=jax.ShapeDtypeStruct((t, dm), F32),
        scratch_shapes=[pltpu.VMEM((2, nt * PEER_SLOTS, dm), jnp.uint32),
                        pltpu.SemaphoreType.DMA((2,))],
        compiler_params=_params(1), name="peer")(eid, eid, h2, gate, x1, g2, uv_bf)


SUBLANES = 8
ROW_TILES = 8


def _dot_f32_rhs01(a, b01):
    hi = a.astype(BF16)
    r1 = a - hi.astype(F32)
    mid = r1.astype(BF16)
    lo = (r1 - mid.astype(F32)).astype(BF16)
    d = lambda x: jnp.dot(x, b01, preferred_element_type=F32)
    return d(hi) + d(mid) + d(lo)


def _peer_t_kernel(eid0_ref, eidn_ref, h8_ref, gate_ref, x8_ref, g8_ref, et_ref, e_ref, uv_hbm, o_ref, buf, sem):
    i = pl.program_id(0)
    n_tok = h8_ref.shape[0]
    n_rows = n_tok * PEER_SLOTS
    slot = i & 1
    dm = ROW_TILES * LANES

    def issue(eid_ref, dst_slot):
        def body(r2, _):
            for par in range(2):
                r = 2 * r2 + par
                e = eid_ref[lax.shift_right_logical(r, 7), r & (PEER_SLOTS - 1)]
                src = uv_hbm.at[pl.ds(pl.multiple_of(e * ROW_TILES, ROW_TILES), ROW_TILES)]
                dst = buf.at[dst_slot, pl.ds(pl.multiple_of(r * ROW_TILES, ROW_TILES), ROW_TILES)]
                pltpu.make_async_copy(src, dst, sem.at[dst_slot]).start(priority=par)
            return 0
        lax.fori_loop(0, n_rows // 2, body, 0, unroll=4)

    @pl.when(i == 0)
    def _():
        issue(eid0_ref, 0)

    @pl.when(i + 1 < pl.num_programs(0))
    def _():
        issue(eidn_ref, 1 - slot)

    pltpu.make_async_copy(uv_hbm.at[pl.ds(0, n_rows * ROW_TILES)], buf.at[slot], sem.at[slot]).wait()

    col = lax.broadcasted_iota(jnp.int32, (SUBLANES, dm), 1)
    diag = (col & (ROW_TILES - 1)) == lax.broadcasted_iota(jnp.int32, (SUBLANES, dm), 0)
    row = lax.broadcasted_iota(jnp.int32, (SUBLANES, 1), 0)
    tok_rows = PEER_SLOTS * ROW_TILES
    for g in range(n_tok // SUBLANES):
        partial = jnp.zeros((SUBLANES, dm), F32)
        for tt in range(SUBLANES):
            t = g * SUBLANES + tt
            w = buf[slot, pl.ds(t * tok_rows, tok_rows), :]
            u_t = lax.bitcast_convert_type(w << 16, F32).astype(BF16)
            p = lax.dot_general(h8_ref[t].astype(BF16), u_t, NT_DIMS, preferred_element_type=F32)
            r = jnp.sum(jnp.where(diag, p, 0.0), axis=0, keepdims=True)
            partial = jnp.where(row == tt, r, partial)
        act = _gelu_tanh(_dot_f32_rhs01(partial, et_ref[...]))
        wgt = (gate_ref[g * SUBLANES:(g + 1) * SUBLANES, :] * act).astype(BF16)
        w_rep = jnp.dot(wgt, e_ref[...], preferred_element_type=F32)
        for tt in range(SUBLANES):
            t = g * SUBLANES + tt
            w = buf[slot, pl.ds(t * tok_rows, tok_rows), :]
            v_t = lax.bitcast_convert_type(w & jnp.uint32(0xFFFF0000), F32).astype(BF16)
            w8 = jnp.where(diag, w_rep[tt:tt + 1], 0.0).astype(BF16)
            out = jnp.dot(w8, v_t, preferred_element_type=F32)
            g8 = g8_ref[0] if g8_ref.shape[0] == 1 else g8_ref[t]
            o_ref[t] = x8_ref[t] + g8 * out


def _peer_t(eid, h2, gate, x1, g2, uv_tiles, rows_per_mod):
    t, dm = h2.shape
    nt = PEER_STEP_TOKENS
    n_steps = t // nt
    assert t % nt == 0 and dm == ROW_TILES * LANES and uv_tiles.shape[1] == LANES
    tile3 = lambda a: a.reshape(-1, ROW_TILES, LANES)
    if rows_per_mod == 1:
        g_spec = pl.BlockSpec((nt, ROW_TILES, LANES), lambda i: (i, 0, 0))
    else:
        assert rows_per_mod % nt == 0
        g_spec = pl.BlockSpec((1, ROW_TILES, LANES), lambda i: ((i * nt) // rows_per_mod, 0, 0))
    tok3 = pl.BlockSpec((nt, ROW_TILES, LANES), lambda i: (i, 0, 0))
    col = jnp.arange(dm, dtype=jnp.int32)
    expand = (col[None, :] // ROW_TILES == jnp.arange(PEER_SLOTS, dtype=jnp.int32)[:, None]).astype(BF16)
    const = lambda a: pl.BlockSpec(a.shape, lambda i: (0, 0))
    out = pl.pallas_call(
        _peer_t_kernel, grid=(n_steps,),
        in_specs=[pl.BlockSpec((nt, PEER_SLOTS), lambda i: (i, 0), memory_space=pltpu.SMEM),
                  pl.BlockSpec((nt, PEER_SLOTS), lambda i: (jnp.minimum(i + 1, n_steps - 1), 0),
                               memory_space=pltpu.SMEM),
                  tok3, pl.BlockSpec((nt, PEER_SLOTS), lambda i: (i, 0)), tok3, g_spec,
                  const(expand.T), const(expand), pl.BlockSpec(memory_space=pl.ANY)],
        out_specs=tok3,
        out_shape=jax.ShapeDtypeStruct((t, ROW_TILES, LANES), F32),
        scratch_shapes=[pltpu.VMEM((2, nt * PEER_SLOTS * ROW_TILES, LANES), jnp.uint32),
                        pltpu.SemaphoreType.DMA((2,))],
        compiler_params=_params(1), name="peer")(
            eid, eid, tile3(h2), gate, tile3(x1), tile3(g2), expand.T, expand, uv_tiles)
    return out.reshape(t, dm)


def _pack_bf16_pair(lo, hi):
    bits = lambda a: lax.bitcast_convert_type(a.astype(BF16), jnp.uint16).astype(jnp.uint32)
    return bits(lo) | (bits(hi) << 16)


def _channel_mix(x, ao, d, ga, gb, mods, wts):
    g1, sh2, sc2, g2 = mods
    bsz, s, dm = x.shape
    x1, h2 = _mix(x, ao, d, ga, gb, g1, sc2, sh2, wts["norm2_g"], wts["w_pool"], wts["pool_scale"],
                  wts["w_branch_attn"], wts["w_branch_pool"], wts["w_out"])
    h2f = h2.reshape(bsz * s, dm)
    eid, gate = _route_t(h2f, wts["peer_wq"], wts["peer_subkeys"])
    per_token = g2.shape[1] != 1
    g2f = g2.reshape(-1, 1, dm)
    out = _peer_t(eid, h2f, gate, x1.reshape(bsz * s, dm), g2f, wts["peer_uv"], 1 if per_token else s)
    return out.reshape(bsz, s, dm)


def kernel(x_prompt, x_sample, cache_k, cache_v, state_pool, c_prompt, c_sample, norm1_g, norm2_g, w_ada, b_ada, w_in, q_norm_g, k_norm_g, attn_sinks, w_pool, pool_scale, w_branch_attn, w_branch_pool, w_out, peer_wq, peer_subkeys, peer_u, peer_v):
    depth = w_ada.shape[0]
    bsz, seq, dm = x_prompt.shape
    nd = x_sample.shape[0]
    assert x_sample.shape[1] == 1
    yp = x_prompt
    ys = x_sample.reshape(1, nd, dm)
    kp, vp, pp, ksm, vsm, psm = [], [], [], [], [], []
    for l in range(depth):
        wts = {
            "norm2_g": norm2_g[l], "pool_scale": pool_scale[l],
            "w_pool": w_pool[l].astype(BF16), "w_branch_attn": w_branch_attn[l].astype(BF16),
            "w_branch_pool": w_branch_pool[l].astype(BF16), "w_out": w_out[l].astype(BF16),
            "peer_wq": peer_wq[l].astype(BF16),
            "peer_subkeys": peer_subkeys[l].reshape(2 * PEER_HEADS, PEER_NKEYS, -1).astype(BF16),
            "peer_uv": _pack_bf16_pair(peer_u[l], peer_v[l]).reshape(-1, LANES),
        }
        w_in_bf = w_in[l].astype(BF16)
        m = _ada(jnp.concatenate([c_prompt, c_sample], axis=0), w_ada[l].astype(BF16), b_ada[l])
        m6 = [m[:, i * dm:(i + 1) * dm] for i in range(6)]
        mp = [a[:bsz].reshape(bsz, 1, dm) for a in m6]
        ms = [a[bsz:].reshape(1, nd, dm) for a in m6]

        q, k, v, p, ga, gb = _in_proj(yp, mp[1], mp[0], norm1_g[l], w_in_bf)
        ao, kn = _attn_prompt(q, k, v, q_norm_g[l], k_norm_g[l], attn_sinks[l])
        kp.append(kn[:, -WINDOW:].reshape(bsz, WINDOW, N_KV_HEADS, HEAD_DIM))
        vp.append(v[:, -WINDOW:].reshape(bsz, WINDOW, N_KV_HEADS, HEAD_DIM))
        pp.append(p[:, -POOL_HIST:])
        yp = _channel_mix(yp, ao, _pool_prompt(p), ga, gb, (mp[2], mp[3], mp[4], mp[5]), wts)

        q, k, v, p, ga, gb = _in_proj(ys, ms[1], ms[0], norm1_g[l], w_in_bf)
        ck = cache_k[l]
        cv = cache_v[l]
        win = ck.shape[1]
        ao, kn = _attn_sample(q.reshape(nd, N_HEADS, HEAD_DIM), k.reshape(nd, N_KV_HEADS, HEAD_DIM),
                              v.reshape(nd, N_KV_HEADS, HEAD_DIM), ck.reshape(nd, win, KV_W),
                              cv.reshape(nd, win, KV_W), q_norm_g[l], k_norm_g[l], attn_sinks[l])
        ksm.append(jnp.concatenate([ck, kn[:, None]], axis=1)[:, -win:])
        vsm.append(jnp.concatenate([cv, v.reshape(nd, 1, N_KV_HEADS, HEAD_DIM)], axis=1)[:, -win:])
        st = state_pool[l]
        psm.append(jnp.concatenate([st, p.reshape(nd, 1, POOL_W)], axis=1)[:, -st.shape[1]:])
        d = _pool_sample(p.reshape(nd, POOL_W), st).reshape(1, nd, POOL_W)
        ys = _channel_mix(ys, ao.reshape(1, nd, ATT_W), d, ga, gb, (ms[2], ms[3], ms[4], ms[5]), wts)
    return (yp, ys.reshape(nd, 1, dm), jnp.stack(kp), jnp.stack(vp), jnp.stack(pp),
            jnp.stack(ksm), jnp.stack(vsm), jnp.stack(psm))
```

```python
import functools

import jax
import jax.numpy as jnp
from jax import lax
from jax.experimental import pallas as pl
from jax.experimental.pallas import tpu as pltpu

F32 = jnp.float32
BF16 = jnp.bfloat16
NORM_EPS = 1e-6
N_HEADS = 8
N_KV_HEADS = 2
Q_GROUP = N_HEADS // N_KV_HEADS
HEAD_DIM = 64
WINDOW = 128
ATT_W = N_HEADS * HEAD_DIM
KV_W = N_KV_HEADS * HEAD_DIM
POOL_WINDOWS = (2, 4, 8, 16)
POOL_GC = 128
POOL_W = POOL_GC * len(POOL_WINDOWS)
POOL_HIST = max(POOL_WINDOWS) - 1
HIST_ROWS = 16
PEER_HEADS = 8
PEER_NKEYS = 128
PEER_TOPK = 16
PEER_SLOTS = PEER_HEADS * PEER_TOPK
PEER_TOKENS = 8
LANES = 128
VMEM_LIMIT_BYTES = 48 * 1024 * 1024
NEG_INF = float("-inf")
NT_DIMS = (((1,), (1,)), ((), ()))


def _params(n_axes):
    return pltpu.CompilerParams(dimension_semantics=("arbitrary",) * n_axes,
                                vmem_limit_bytes=VMEM_LIMIT_BYTES)


def _rms(x, g):
    return x * lax.rsqrt(jnp.mean(x * x, axis=-1, keepdims=True) + NORM_EPS) * g


def _bdot(a, b):
    return jnp.dot(a.astype(BF16), b.astype(BF16), preferred_element_type=F32)


def _bdot_nt(a, b):
    return lax.dot_general(a.astype(BF16), b.astype(BF16), NT_DIMS, preferred_element_type=F32)


def _ada_kernel(c_ref, w_ref, b_ref, o_ref):
    c = c_ref[...]
    o_ref[...] = _bdot(c * jax.nn.sigmoid(c), w_ref[...]) + b_ref[...]


def _ada(c, w_bf, b):
    n, d = c.shape
    nout = w_bf.shape[1]
    tn = nout // 4
    return pl.pallas_call(
        _ada_kernel, grid=(nout // tn,),
        in_specs=[pl.BlockSpec((n, d), lambda j: (0, 0)),
                  pl.BlockSpec((d, tn), lambda j: (0, j)),
                  pl.BlockSpec((1, tn), lambda j: (0, j))],
        out_specs=pl.BlockSpec((n, tn), lambda j: (0, j)),
        out_shape=jax.ShapeDtypeStruct((n, nout), F32),
        compiler_params=_params(1), name="ada")(c, w_bf, b.reshape(1, nout))


_IN_CUTS = (0, ATT_W, ATT_W + KV_W, ATT_W + 2 * KV_W, ATT_W + 2 * KV_W + POOL_W)


def _in_kernel(x_ref, sc_ref, sh_ref, g_ref, w_ref, q_ref, k_ref, v_ref, p_ref, ga_ref, gb_ref):
    d = x_ref.shape[-1]
    h = _rms(x_ref[0], g_ref[...]) * (1.0 + sc_ref[0]) + sh_ref[0]
    hb = h.astype(BF16)
    cuts = _IN_CUTS + (_IN_CUTS[-1] + d, _IN_CUTS[-1] + 2 * d)
    for o_ref, lo, hi in zip((q_ref, k_ref, v_ref, p_ref, ga_ref, gb_ref), cuts[:-1], cuts[1:]):
        o_ref[0] = jnp.dot(hb, w_ref[:, lo:hi], preferred_element_type=F32)


def _mod_spec(mod, tm):
    if mod.shape[1] == 1:
        return pl.BlockSpec((1, 1, mod.shape[2]), lambda b, j: (b, 0, 0))
    return pl.BlockSpec((1, tm, mod.shape[2]), lambda b, j: (b, j, 0))


def _tile(s):
    return min(256, s)


def _in_proj(x, sc, sh, g, w_bf):
    bsz, s, d = x.shape
    tm = _tile(s)
    widths = (ATT_W, KV_W, KV_W, POOL_W, d, d)
    tok = lambda w: pl.BlockSpec((1, tm, w), lambda b, j: (b, j, 0))
    return pl.pallas_call(
        _in_kernel, grid=(bsz, s // tm),
        in_specs=[tok(d), _mod_spec(sc, tm), _mod_spec(sh, tm),
                  pl.BlockSpec((1, d), lambda b, j: (0, 0)),
                  pl.BlockSpec(w_bf.shape, lambda b, j: (0, 0))],
        out_specs=[tok(w) for w in widths],
        out_shape=[jax.ShapeDtypeStruct((bsz, s, w), F32) for w in widths],
        compiler_params=_params(2), name="in_proj")(x, sc, sh, g.reshape(1, d), w_bf)


def _attn_p_kernel(sinks_ref, q_ref, kc_ref, kp_ref, vc_ref, vp_ref, qg_ref, kg_ref, o_ref, kn_ref):
    j = pl.program_id(1)
    blk = q_ref.shape[1]
    qi = lax.broadcasted_iota(jnp.int32, (blk, 2 * blk), 0)
    kj = lax.broadcasted_iota(jnp.int32, (blk, 2 * blk), 1)
    dist = qi + blk - kj
    valid = (dist >= 0) & (dist <= WINDOW) & ((kj >= blk) | (j > 0))
    dist_f = dist.astype(F32)
    qg = qg_ref[...]
    kg = kg_ref[...]
    for kv in range(N_KV_HEADS):
        ks = slice(kv * HEAD_DIM, (kv + 1) * HEAD_DIM)
        kcn = _rms(kc_ref[0, :, ks], kg)
        kn_ref[0, :, ks] = kcn
        kk = jnp.concatenate([_rms(kp_ref[0, :, ks], kg), kcn], axis=0).astype(BF16)
        vv = jnp.concatenate([vp_ref[0, :, ks], vc_ref[0, :, ks]], axis=0).astype(BF16)
        for g in range(Q_GROUP):
            h = kv * Q_GROUP + g
            hs = slice(h * HEAD_DIM, (h + 1) * HEAD_DIM)
            qn = _rms(q_ref[0, :, hs], qg)
            s = _bdot_nt(qn, kk) * (HEAD_DIM ** -0.5)
            s = s - (2.0 ** -(h + 1)) * dist_f
            s = jnp.where(valid, s, -1e30)
            sink = sinks_ref[h]
            m = jnp.maximum(jnp.max(s, axis=-1, keepdims=True), sink)
            e = jnp.exp(s - m)
            den = jnp.sum(e, axis=-1, keepdims=True) + jnp.exp(sink - m)
            o_ref[0, :, hs] = jnp.dot((e / den).astype(BF16), vv, preferred_element_type=F32)


def _attn_prompt(q, k, v, qg, kg, sinks):
    bsz, s, _ = q.shape
    blk = WINDOW
    cur = lambda w: pl.BlockSpec((1, blk, w), lambda b, j: (b, j, 0))
    prev = lambda w: pl.BlockSpec((1, blk, w), lambda b, j: (b, jnp.maximum(j - 1, 0), 0))
    vec = pl.BlockSpec((1, HEAD_DIM), lambda b, j: (0, 0))
    return pl.pallas_call(
        _attn_p_kernel, grid=(bsz, s // blk),
        in_specs=[pl.BlockSpec(memory_space=pltpu.SMEM), cur(ATT_W), cur(KV_W), prev(KV_W),
                  cur(KV_W), prev(KV_W), vec, vec],
        out_specs=[cur(ATT_W), cur(KV_W)],
        out_shape=[jax.ShapeDtypeStruct((bsz, s, ATT_W), F32),
                   jax.ShapeDtypeStruct((bsz, s, KV_W), F32)],
        compiler_params=_params(2), name="attn_prompt")(
            sinks, q, k, k, v, v, qg.reshape(1, HEAD_DIM), kg.reshape(1, HEAD_DIM))


def _row_consts(vals):
    row = lax.broadcasted_iota(jnp.int32, (N_HEADS, 1), 0)
    out = jnp.zeros((N_HEADS, 1), F32)
    for h, v in enumerate(vals):
        out = jnp.where(row == h, v, out)
    return out


def _attn_s_kernel(sinks_ref, q_ref, kx_ref, vx_ref, ck_ref, cv_ref, qg_ref, kg_ref, o_ref, kn_ref):
    nb = q_ref.shape[0]
    w = ck_ref.shape[1]
    slope = _row_consts([2.0 ** -(h + 1) for h in range(N_HEADS)])
    sink = _row_consts([sinks_ref[h] for h in range(N_HEADS)])
    row = lax.broadcasted_iota(jnp.int32, (N_HEADS, 1), 0)
    first = row < Q_GROUP
    dist_c = (w - lax.broadcasted_iota(jnp.int32, (1, w), 1)).astype(F32)
    scale = HEAD_DIM ** -0.5
    for b in range(nb):
        qn = _rms(q_ref[b], qg_ref[...])
        kxn = _rms(kx_ref[b], kg_ref[...])
        kn_ref[b] = kxn
        vx = vx_ref[b]
        ck = ck_ref[b]
        cv = cv_ref[b]
        kx_h = jnp.where(first, kxn[0:1], kxn[1:2])
        vx_h = jnp.where(first, vx[0:1], vx[1:2])
        s_c = jnp.where(first, _bdot_nt(qn, ck[:, :HEAD_DIM]), _bdot_nt(qn, ck[:, HEAD_DIM:]))
        s_c = s_c * scale - slope * dist_c
        s_n = jnp.sum(qn * kx_h, axis=-1, keepdims=True) * scale
        m = jnp.maximum(jnp.maximum(jnp.max(s_c, axis=-1, keepdims=True), s_n), sink)
        e_c = jnp.exp(s_c - m)
        e_n = jnp.exp(s_n - m)
        den = jnp.sum(e_c, axis=-1, keepdims=True) + e_n + jnp.exp(sink - m)
        p_c = e_c / den
        o_c = jnp.where(first, _bdot(p_c, cv[:, :HEAD_DIM]), _bdot(p_c, cv[:, HEAD_DIM:]))
        o_ref[b] = o_c + (e_n / den) * vx_h


def _attn_sample(q, kx, vx, ck, cv, qg, kg, sinks):
    n = q.shape[0]
    w = ck.shape[1]
    nb = 8
    blk = lambda a, c: pl.BlockSpec((nb, a, c), lambda i: (i, 0, 0))
    vec = pl.BlockSpec((1, HEAD_DIM), lambda i: (0, 0))
    return pl.pallas_call(
        _attn_s_kernel, grid=(n // nb,),
        in_specs=[pl.BlockSpec(memory_space=pltpu.SMEM), blk(N_HEADS, HEAD_DIM),
                  blk(N_KV_HEADS, HEAD_DIM), blk(N_KV_HEADS, HEAD_DIM), blk(w, KV_W), blk(w, KV_W),
                  vec, vec],
        out_specs=[blk(N_HEADS, HEAD_DIM), blk(N_KV_HEADS, HEAD_DIM)],
        out_shape=[jax.ShapeDtypeStruct((n, N_HEADS, HEAD_DIM), F32),
                   jax.ShapeDtypeStruct((n, N_KV_HEADS, HEAD_DIM), F32)],
        compiler_params=_params(1), name="attn_sample")(
            sinks, q, kx, vx, ck, cv, qg.reshape(1, HEAD_DIM), kg.reshape(1, HEAD_DIM))


def _pool_p_kernel(p_ref, hist_ref, d_ref, ext_ref):
    j = pl.program_id(1)
    tm = p_ref.shape[1]
    cur = p_ref[0]
    ext_ref[0:HIST_ROWS, :] = hist_ref[0, 0]
    ext_ref[HIST_ROWS:, :] = cur
    pos = j * tm + lax.broadcasted_iota(jnp.int32, (tm, 1), 0)
    acc = cur
    lo = 1
    for gi, w in enumerate(POOL_WINDOWS):
        for s in range(lo, w):
            acc = acc + ext_ref[HIST_ROWS - s:HIST_ROWS - s + tm, :]
        lo = w
        gs = slice(gi * POOL_GC, (gi + 1) * POOL_GC)
        cnt = jnp.minimum(pos + 1, w).astype(F32)
        d_ref[0, :, gs] = acc[:, gs] / cnt - cur[:, gs]


def _pool_prompt(p):
    bsz, s, pw = p.shape
    tm = _tile(s)
    nt = s // tm
    tails = p.reshape(bsz, nt, tm, pw)[:, :, tm - HIST_ROWS:, :]
    hist = jnp.concatenate([jnp.zeros((bsz, 1, HIST_ROWS, pw), F32), tails[:, :-1]], axis=1)
    return pl.pallas_call(
        _pool_p_kernel, grid=(bsz, nt),
        in_specs=[pl.BlockSpec((1, tm, pw), lambda b, j: (b, j, 0)),
                  pl.BlockSpec((1, 1, HIST_ROWS, pw), lambda b, j: (b, j, 0, 0))],
        out_specs=pl.BlockSpec((1, tm, pw), lambda b, j: (b, j, 0)),
        out_shape=jax.ShapeDtypeStruct((bsz, s, pw), F32),
        scratch_shapes=[pltpu.VMEM((tm + HIST_ROWS, pw), F32)],
        compiler_params=_params(2), name="pool_prompt")(p, hist)


def _pool_s_kernel(p_ref, st_ref, d_ref):
    n_hist = st_ref.shape[1]
    cur = p_ref[...]
    acc = cur
    lo = 1
    for gi, w in enumerate(POOL_WINDOWS):
        for s in range(lo, w):
            acc = acc + st_ref[:, n_hist - s, :]
        lo = w
        gs = slice(gi * POOL_GC, (gi + 1) * POOL_GC)
        d_ref[:, gs] = acc[:, gs] / float(w) - cur[:, gs]


def _pool_sample(p, state):
    assert state.shape[1] == POOL_HIST
    return pl.pallas_call(
        _pool_s_kernel, out_shape=jax.ShapeDtypeStruct(p.shape, F32),
        compiler_params=pltpu.CompilerParams(vmem_limit_bytes=VMEM_LIMIT_BYTES),
        name="pool_sample")(p, state)


def _mix_kernel(x_ref, ao_ref, d_ref, ga_ref, gb_ref, g1_ref, sc2_ref, sh2_ref, n2_ref,
                wp_ref, ps_ref, wba_ref, wbp_ref, wo_ref, x1_ref, h2_ref):
    d = d_ref[0]
    po = jnp.concatenate(
        [_bdot(d[:, gi * POOL_GC:(gi + 1) * POOL_GC], wp_ref[gi]) for gi in range(len(POOL_WINDOWS))],
        axis=-1) * ps_ref[...]
    merged = (jax.nn.sigmoid(ga_ref[0]) * _bdot(ao_ref[0], wba_ref[...])
              + jax.nn.sigmoid(gb_ref[0]) * _bdot(po, wbp_ref[...]))
    x1 = x_ref[0] + g1_ref[0] * _bdot(merged, wo_ref[...])
    x1_ref[0] = x1
    h2_ref[0] = _rms(x1, n2_ref[...]) * (1.0 + sc2_ref[0]) + sh2_ref[0]


def _mix(x, ao, d, ga, gb, g1, sc2, sh2, n2g, wp_bf, pscale, wba_bf, wbp_bf, wo_bf):
    bsz, s, dm = x.shape
    tm = _tile(s)
    tok = lambda w: pl.BlockSpec((1, tm, w), lambda b, j: (b, j, 0))
    full = lambda a: pl.BlockSpec(a.shape, lambda b, j: (0,) * a.ndim)
    n2g = n2g.reshape(1, dm)
    pscale = pscale.reshape(1, POOL_W)
    return pl.pallas_call(
        _mix_kernel, grid=(bsz, s // tm),
        in_specs=[tok(dm), tok(ATT_W), tok(POOL_W), tok(dm), tok(dm),
                  _mod_spec(g1, tm), _mod_spec(sc2, tm), _mod_spec(sh2, tm), full(n2g),
                  full(wp_bf), full(pscale), full(wba_bf), full(wbp_bf), full(wo_bf)],
        out_specs=[tok(dm), tok(dm)],
        out_shape=[jax.ShapeDtypeStruct((bsz, s, dm), F32)] * 2,
        compiler_params=_params(2), name="mix")(
            x, ao, d, ga, gb, g1, sc2, sh2, n2g, wp_bf, pscale, wba_bf, wbp_bf, wo_bf)


def _route_kernel(h2_ref, wq_ref, sk_ref, eid_ref, gate_ref, sc_ref):
    tm = h2_ref.shape[0]
    hb = h2_ref[...].astype(BF16)
    for hc in range(2 * PEER_HEADS):
        q = jnp.dot(hb, wq_ref[:, hc * LANES:(hc + 1) * LANES], preferred_element_type=F32)
        sc_ref[hc] = _bdot_nt(q, sk_ref[hc])
    lane = lax.broadcasted_iota(jnp.int32, (tm, LANES), 1)
    lane_f = lane.astype(F32)
    lane_hi = lax.shift_right_logical(lane, 4)
    lane_lo = lane & (PEER_TOPK - 1)
    zeros = jnp.zeros((tm, LANES), F32)
    half = LANES // PEER_TOPK

    def take_max(s):
        m = jnp.max(s, axis=-1, keepdims=True)
        idx = jnp.min(jnp.where(s == m, lane_f, float(LANES)), axis=-1, keepdims=True)
        return m, idx, jnp.where(lane_f == idx, NEG_INF, s)

    def head_body(h, _):
        def first_half(i, c):
            s, v_lo, v_hi, i_lo, i_hi = c
            m, idx, s = take_max(s)
            at_lo = lane_hi == i
            at_hi = lane_hi == i - half
            return (s, jnp.where(at_lo, m, v_lo), jnp.where(at_hi, m, v_hi),
                    jnp.where(at_lo, idx, i_lo), jnp.where(at_hi, idx, i_hi))

        def second_half(j, c):
            s, v, ix = c
            m, idx, s = take_max(s)
            at = lane_lo == j
            return s, jnp.where(at, m, v), jnp.where(at, idx, ix)

        _, a_lo, a_hi, ia_lo, ia_hi = lax.fori_loop(
            0, PEER_TOPK, first_half, (sc_ref[2 * h], zeros, zeros, zeros, zeros))
        _, b_t, ib_t = lax.fori_loop(0, PEER_TOPK, second_half, (sc_ref[2 * h + 1], zeros, zeros))
        e_lo = ia_lo * float(PEER_NKEYS) + ib_t
        e_hi = ia_hi * float(PEER_NKEYS) + ib_t

        def pick(k, c):
            c_lo, c_hi, best, eids = c
            m = jnp.maximum(jnp.max(c_lo, axis=-1, keepdims=True), jnp.max(c_hi, axis=-1, keepdims=True))
            p_lo = jnp.min(jnp.where(c_lo == m, lane_f, 2.0 * LANES), axis=-1, keepdims=True)
            p_hi = jnp.min(jnp.where(c_hi == m, lane_f + LANES, 2.0 * LANES), axis=-1, keepdims=True)
            p = jnp.minimum(p_lo, p_hi)
            sel_lo = lane_f == p
            sel_hi = lane_f + LANES == p
            e = jnp.maximum(jnp.max(jnp.where(sel_lo, e_lo, -1.0), axis=-1, keepdims=True),
                            jnp.max(jnp.where(sel_hi, e_hi, -1.0), axis=-1, keepdims=True))
            at = lane == k
            return (jnp.where(sel_lo, NEG_INF, c_lo), jnp.where(sel_hi, NEG_INF, c_hi),
                    jnp.where(at, m, best), jnp.where(at, e, eids))

        _, _, best, eids = lax.fori_loop(0, PEER_TOPK, pick, (a_lo + b_t, a_hi + b_t, zeros, zeros))
        live = lane < PEER_TOPK
        mx = jnp.max(jnp.where(live, best, NEG_INF), axis=-1, keepdims=True)
        ex = jnp.where(live, jnp.exp(best - mx), 0.0)
        gate_ref[h] = ex / jnp.sum(ex, axis=-1, keepdims=True)
        eid_ref[h] = eids.astype(jnp.int32)
        return 0

    lax.fori_loop(0, PEER_HEADS, head_body, 0)


def _route(h2, wq_bf, sk_bf):
    t, dm = h2.shape
    tm = min(128, t)
    out = pl.BlockSpec((PEER_HEADS, tm, LANES), lambda i: (0, i, 0))
    eid, gate = pl.pallas_call(
        _route_kernel, grid=(t // tm,),
        in_specs=[pl.BlockSpec((tm, dm), lambda i: (i, 0)),
                  pl.BlockSpec(wq_bf.shape, lambda i: (0, 0)),
                  pl.BlockSpec(sk_bf.shape, lambda i: (0, 0, 0))],
        out_specs=[out, out],
        out_shape=[jax.ShapeDtypeStruct((PEER_HEADS, t, LANES), jnp.int32),
                   jax.ShapeDtypeStruct((PEER_HEADS, t, LANES), F32)],
        scratch_shapes=[pltpu.VMEM((2 * PEER_HEADS, tm, LANES), F32)],
        compiler_params=_params(1), name="route")(h2, wq_bf, sk_bf)
    flat = lambda a: jnp.transpose(a[:, :, :PEER_TOPK], (1, 0, 2)).reshape(t, PEER_SLOTS)
    return flat(eid), flat(gate)


def _gelu_tanh(x):
    return x * (0.5 * (1.0 + jnp.tanh(0.7978845608028654 * (x + 0.044715 * (x * x * x)))))


def _peer_kernel(eid_ref, h2_ref, gate_ref, x1_ref, g2_ref, u_hbm, v_hbm, o_ref, ubuf, vbuf, sem):
    n_tok = h2_ref.shape[0]

    def row_copies(t, k):
        e = eid_ref[t, k]
        r = t * PEER_SLOTS + k
        return (pltpu.make_async_copy(u_hbm.at[pl.ds(e, 1)], ubuf.at[pl.ds(r, 1)], sem.at[0]),
                pltpu.make_async_copy(v_hbm.at[pl.ds(e, 1)], vbuf.at[pl.ds(r, 1)], sem.at[1]))

    def start_row(r, _):
        cu, cv = row_copies(lax.shift_right_logical(r, 7), r & (PEER_SLOTS - 1))
        cu.start()
        cv.start()
        return 0

    def wait_row(r, _):
        cu, cv = row_copies(lax.shift_right_logical(r, 7), r & (PEER_SLOTS - 1))
        cu.wait()
        cv.wait()
        return 0

    lax.fori_loop(0, n_tok * PEER_SLOTS, start_row, 0)
    lax.fori_loop(0, n_tok * PEER_SLOTS, wait_row, 0)

    hb = h2_ref[...].astype(BF16)
    gate = gate_ref[...]
    row = lax.broadcasted_iota(jnp.int32, (n_tok, 1), 0)
    acc = jnp.zeros(o_ref.shape, F32)
    for t in range(n_tok):
        rows = pl.ds(t * PEER_SLOTS, PEER_SLOTS)
        act = _gelu_tanh(_bdot_nt(hb, ubuf[rows, :]))
        out_t = _bdot(gate * act, vbuf[rows, :])
        acc = jnp.where(row == t, out_t, acc)
    o_ref[...] = x1_ref[...] + g2_ref[:, 0, :] * acc


def _peer(eid, h2, gate, x1, g2, u_tab, v_tab, rows_per_mod):
    t, dm = h2.shape
    nt = PEER_TOKENS
    if rows_per_mod == 1:
        g2_spec = pl.BlockSpec((nt, 1, dm), lambda i: (i, 0, 0))
    else:
        assert rows_per_mod % nt == 0
        g2_spec = pl.BlockSpec((1, 1, dm), lambda i: ((i * nt) // rows_per_mod, 0, 0))
    tok = lambda w: pl.BlockSpec((nt, w), lambda i: (i, 0))
    return pl.pallas_call(
        _peer_kernel, grid=(t // nt,),
        in_specs=[pl.BlockSpec((nt, PEER_SLOTS), lambda i: (i, 0), memory_space=pltpu.SMEM),
                  tok(dm), tok(PEER_SLOTS), tok(dm), g2_spec,
                  pl.BlockSpec(memory_space=pl.ANY), pl.BlockSpec(memory_space=pl.ANY)],
        out_specs=tok(dm),
        out_shape=jax.ShapeDtypeStruct((t, dm), F32),
        scratch_shapes=[pltpu.VMEM((nt * PEER_SLOTS, dm), F32),
                        pltpu.VMEM((nt * PEER_SLOTS, dm), F32),
                        pltpu.SemaphoreType.DMA((2,))],
        compiler_params=_params(1), name="peer")(eid, h2, gate, x1, g2, u_tab, v_tab)


def _route_t_kernel(h2_ref, wq_ref, sk_ref, eid_ref, gate_ref, sc_ref):
    tm = h2_ref.shape[0]
    hb = h2_ref[...].astype(BF16)
    for hc in range(2 * PEER_HEADS):
        q = jnp.dot(hb, wq_ref[:, hc * LANES:(hc + 1) * LANES], preferred_element_type=F32)
        sc_ref[hc] = _bdot_nt(sk_ref[hc], q)
    key_f = lax.broadcasted_iota(jnp.int32, (PEER_NKEYS, tm), 0).astype(F32)
    n_cand = PEER_TOPK * PEER_TOPK
    cand_f = lax.broadcasted_iota(jnp.int32, (n_cand, tm), 0).astype(F32)
    rank = lax.broadcasted_iota(jnp.int32, (PEER_TOPK, tm), 0)
    zeros = jnp.zeros((PEER_TOPK, tm), F32)

    def top_half(s):
        def body(i, c):
            s, v, ix = c
            m = jnp.max(s, axis=0, keepdims=True)
            idx = jnp.min(jnp.where(s == m, key_f, float(PEER_NKEYS)), axis=0, keepdims=True)
            at = rank == i
            return jnp.where(key_f == idx, NEG_INF, s), jnp.where(at, m, v), jnp.where(at, idx, ix)
        _, v, ix = lax.fori_loop(0, PEER_TOPK, body, (s, zeros, zeros))
        return v, ix

    def head_body(h, _):
        a, ia = top_half(sc_ref[2 * h])
        b, ib = top_half(sc_ref[2 * h + 1])
        cand = jnp.concatenate([a[i:i + 1] + b for i in range(PEER_TOPK)], axis=0)
        expert = jnp.concatenate([ia[i:i + 1] * float(PEER_NKEYS) + ib for i in range(PEER_TOPK)], axis=0)

        def pick(k, c):
            cd, best, eids = c
            m = jnp.max(cd, axis=0, keepdims=True)
            p = jnp.min(jnp.where(cd == m, cand_f, float(n_cand)), axis=0, keepdims=True)
            sel = cand_f == p
            e = jnp.max(jnp.where(sel, expert, -1.0), axis=0, keepdims=True)
            at = rank == k
            return jnp.where(sel, NEG_INF, cd), jnp.where(at, m, best), jnp.where(at, e, eids)

        _, best, eids = lax.fori_loop(0, PEER_TOPK, pick, (cand, zeros, zeros))
        ex = jnp.exp(best - jnp.max(best, axis=0, keepdims=True))
        gate_ref[h] = ex / jnp.sum(ex, axis=0, keepdims=True)
        eid_ref[h] = eids.astype(jnp.int32)
        return 0

    lax.fori_loop(0, PEER_HEADS, head_body, 0)


def _route_t(h2, wq_bf, sk_bf):
    t, dm = h2.shape
    tm = min(LANES, t)
    out = pl.BlockSpec((PEER_HEADS, PEER_TOPK, tm), lambda i: (0, 0, i))
    eid, gate = pl.pallas_call(
        _route_t_kernel, grid=(t // tm,),
        in_specs=[pl.BlockSpec((tm, dm), lambda i: (i, 0)),
                  pl.BlockSpec(wq_bf.shape, lambda i: (0, 0)),
                  pl.BlockSpec(sk_bf.shape, lambda i: (0, 0, 0))],
        out_specs=[out, out],
        out_shape=[jax.ShapeDtypeStruct((PEER_HEADS, PEER_TOPK, t), jnp.int32),
                   jax.ShapeDtypeStruct((PEER_HEADS, PEER_TOPK, t), F32)],
        scratch_shapes=[pltpu.VMEM((2 * PEER_HEADS, PEER_NKEYS, tm), F32)],
        compiler_params=_params(1), name="route")(h2, wq_bf, sk_bf)
    return eid.reshape(PEER_SLOTS, t).T, gate.reshape(PEER_SLOTS, t).T


PEER_STEP_TOKENS = 16
PEER_GROUP = 8


def _peer_db_kernel(eid0_ref, eidn_ref, h2_ref, gate_ref, x1_ref, g2_ref, uv_hbm, o_ref, buf, sem):
    i = pl.program_id(0)
    n_tok, dm = h2_ref.shape
    n_rows = n_tok * PEER_SLOTS
    slot = i & 1

    def issue(eid_ref, dst_slot):
        def body(r2, _):
            for par in range(2):
                r = 2 * r2 + par
                e = eid_ref[lax.shift_right_logical(r, 7), r & (PEER_SLOTS - 1)]
                pltpu.make_async_copy(uv_hbm.at[pl.ds(e, 1)], buf.at[dst_slot, pl.ds(r, 1)],
                                      sem.at[dst_slot]).start(priority=par)
            return 0
        lax.fori_loop(0, n_rows // 2, body, 0, unroll=4)

    @pl.when(i == 0)
    def _():
        issue(eid0_ref, 0)

    @pl.when(i + 1 < pl.num_programs(0))
    def _():
        issue(eidn_ref, 1 - slot)

    pltpu.make_async_copy(uv_hbm.at[pl.ds(0, n_rows)], buf.at[slot], sem.at[slot]).wait()

    row = lax.broadcasted_iota(jnp.int32, (PEER_GROUP, 1), 0)
    g2 = g2_ref[:, 0, :]
    for g in range(n_tok // PEER_GROUP):
        gs = slice(g * PEER_GROUP, (g + 1) * PEER_GROUP)
        hb = h2_ref[gs, :].astype(BF16)
        gate = gate_ref[gs, :]
        acc = jnp.zeros((PEER_GROUP, dm), F32)
        for tt in range(PEER_GROUP):
            rows = pl.ds((g * PEER_GROUP + tt) * PEER_SLOTS, PEER_SLOTS)
            w = buf[slot, rows, :]
            u_t = lax.bitcast_convert_type(w << 16, F32).astype(BF16)
            v_t = lax.bitcast_convert_type(w & jnp.uint32(0xFFFF0000), F32).astype(BF16)
            act = _gelu_tanh(lax.dot_general(hb, u_t, NT_DIMS, preferred_element_type=F32))
            out_t = jnp.dot((gate * act).astype(BF16), v_t, preferred_element_type=F32)
            acc = jnp.where(row == tt, out_t, acc)
        o_ref[gs, :] = x1_ref[gs, :] + (g2 if g2.shape[0] == 1 else g2[gs, :]) * acc


def _peer_db(eid, h2, gate, x1, g2, uv_bf, rows_per_mod):
    t, dm = h2.shape
    nt = PEER_STEP_TOKENS
    n_steps = t // nt
    assert t % nt == 0 and uv_bf.shape[1] == dm and uv_bf.dtype == jnp.uint32
    if rows_per_mod == 1:
        g2_spec = pl.BlockSpec((nt, 1, dm), lambda i: (i, 0, 0))
    else:
        assert rows_per_mod % nt == 0
        g2_spec = pl.BlockSpec((1, 1, dm), lambda i: ((i * nt) // rows_per_mod, 0, 0))
    tok = lambda w: pl.BlockSpec((nt, w), lambda i: (i, 0))
    return pl.pallas_call(
        _peer_db_kernel, grid=(n_steps,),
        in_specs=[pl.BlockSpec((nt, PEER_SLOTS), lambda i: (i, 0), memory_space=pltpu.SMEM),
                  pl.BlockSpec((nt, PEER_SLOTS), lambda i: (jnp.minimum(i + 1, n_steps - 1), 0),
                               memory_space=pltpu.SMEM),
                  tok(dm), tok(PEER_SLOTS), tok(dm), g2_spec, pl.BlockSpec(memory_space=pl.ANY)],
        out_specs=tok(dm),
        out_shape=jax.ShapeDtypeStruct((t, dm), F32),
        scratch_shapes=[pltpu.VMEM((2, nt * PEER_SLOTS, dm), jnp.uint32),
                        pltpu.SemaphoreType.DMA((2,))],
        compiler_params=_params(1), name="peer")(eid, eid, h2, gate, x1, g2, uv_bf)


SUBLANES = 8
ROW_TILES = 8


def _dot_f32_rhs01(a, b01):
    hi = a.astype(BF16)
    r1 = a - hi.astype(F32)
    mid = r1.astype(BF16)
    lo = (r1 - mid.astype(F32)).astype(BF16)
    d = lambda x: jnp.dot(x, b01, preferred_element_type=F32)
    return d(hi) + d(mid) + d(lo)


def _peer_t_kernel(eid0_ref, eidn_ref, h8_ref, gate_ref, x8_ref, g8_ref, et_ref, e_ref, uv_hbm, o_ref, buf, sem):
    i = pl.program_id(0)
    n_tok = h8_ref.shape[0]
    n_rows = n_tok * PEER_SLOTS
    slot = i & 1
    dm = ROW_TILES * LANES

    def issue(eid_ref, dst_slot):
        def body(r2, _):
            for par in range(2):
                r = 2 * r2 + par
                e = eid_ref[0, 0, r]
                pltpu.make_async_copy(uv_hbm.at[e], buf.at[dst_slot, r], sem.at[dst_slot]).start(priority=par)
            return 0
        lax.fori_loop(0, n_rows // 2, body, 0, unroll=4)

    @pl.when(i == 0)
    def _():
        issue(eid0_ref, 0)

    @pl.when(i + 1 < pl.num_programs(0))
    def _():
        issue(eidn_ref, 1 - slot)

    pltpu.make_async_copy(uv_hbm.at[pl.ds(0, n_rows)], buf.at[slot], sem.at[slot]).wait()

    col = lax.broadcasted_iota(jnp.int32, (SUBLANES, dm), 1)
    diag = (col & (ROW_TILES - 1)) == lax.broadcasted_iota(jnp.int32, (SUBLANES, dm), 0)
    row = lax.broadcasted_iota(jnp.int32, (SUBLANES, 1), 0)
    tok_rows = PEER_SLOTS * ROW_TILES
    for g in range(n_tok // SUBLANES):
        partial = jnp.zeros((SUBLANES, dm), F32)
        for tt in range(SUBLANES):
            t = g * SUBLANES + tt
            w = buf[slot, pl.ds(t * PEER_SLOTS, PEER_SLOTS)].reshape(tok_rows, LANES)
            u_t = lax.bitcast_convert_type(w << 16, F32).astype(BF16)
            p = lax.dot_general(h8_ref[t].astype(BF16), u_t, NT_DIMS, preferred_element_type=F32)
            r = jnp.sum(jnp.where(diag, p, 0.0), axis=0, keepdims=True)
            partial = jnp.where(row == tt, r, partial)
        act = _gelu_tanh(_dot_f32_rhs01(partial, et_ref[...]))
        wgt = (gate_ref[g * SUBLANES:(g + 1) * SUBLANES, :] * act).astype(BF16)
        w_rep = jnp.dot(wgt, e_ref[...], preferred_element_type=F32)
        for tt in range(SUBLANES):
            t = g * SUBLANES + tt
            w = buf[slot, pl.ds(t * PEER_SLOTS, PEER_SLOTS)].reshape(tok_rows, LANES)
            v_t = lax.bitcast_convert_type(w & jnp.uint32(0xFFFF0000), F32).astype(BF16)
            w8 = jnp.where(diag, w_rep[tt:tt + 1], 0.0).astype(BF16)
            out = jnp.dot(w8, v_t, preferred_element_type=F32)
            g8 = g8_ref[0] if g8_ref.shape[0] == 1 else g8_ref[t]
            o_ref[t] = x8_ref[t] + g8 * out


def _peer_t(eid, h2, gate, x1, g2, uv_tiles, rows_per_mod):
    t, dm = h2.shape
    nt = PEER_STEP_TOKENS
    n_steps = t // nt
    assert t % nt == 0 and dm == ROW_TILES * LANES and uv_tiles.shape[1] == LANES
    tile3 = lambda a: a.reshape(-1, ROW_TILES, LANES)
    eid = eid.reshape(n_steps, 1, nt * PEER_SLOTS)
    eid_spec = lambda f: pl.BlockSpec((1, 1, nt * PEER_SLOTS), lambda i: (f(i), 0, 0), memory_space=pltpu.SMEM)
    if rows_per_mod == 1:
        g_spec = pl.BlockSpec((nt, ROW_TILES, LANES), lambda i: (i, 0, 0))
    else:
        assert rows_per_mod % nt == 0
        g_spec = pl.BlockSpec((1, ROW_TILES, LANES), lambda i: ((i * nt) // rows_per_mod, 0, 0))
    tok3 = pl.BlockSpec((nt, ROW_TILES, LANES), lambda i: (i, 0, 0))
    col = jnp.arange(dm, dtype=jnp.int32)
    expand = (col[None, :] // ROW_TILES == jnp.arange(PEER_SLOTS, dtype=jnp.int32)[:, None]).astype(BF16)
    const = lambda a: pl.BlockSpec(a.shape, lambda i: (0, 0))
    out = pl.pallas_call(
        _peer_t_kernel, grid=(n_steps,),
        in_specs=[eid_spec(lambda i: i), eid_spec(lambda i: jnp.minimum(i + 1, n_steps - 1)),
                  tok3, pl.BlockSpec((nt, PEER_SLOTS), lambda i: (i, 0)), tok3, g_spec,
                  const(expand.T), const(expand), pl.BlockSpec(memory_space=pl.ANY)],
        out_specs=tok3,
        out_shape=jax.ShapeDtypeStruct((t, ROW_TILES, LANES), F32),
        scratch_shapes=[pltpu.VMEM((2, nt * PEER_SLOTS, ROW_TILES, LANES), jnp.uint32),
                        pltpu.SemaphoreType.DMA((2,))],
        compiler_params=_params(1), name="peer")(
            eid, eid, tile3(h2), gate, tile3(x1), tile3(g2), expand.T, expand, tile3(uv_tiles))
    return out.reshape(t, dm)


def _pack_bf16_pair(lo, hi):
    bits = lambda a: lax.bitcast_convert_type(a.astype(BF16), jnp.uint16).astype(jnp.uint32)
    return bits(lo) | (bits(hi) << 16)


def _channel_mix(x, ao, d, ga, gb, mods, wts):
    g1, sh2, sc2, g2 = mods
    bsz, s, dm = x.shape
    x1, h2 = _mix(x, ao, d, ga, gb, g1, sc2, sh2, wts["norm2_g"], wts["w_pool"], wts["pool_scale"],
                  wts["w_branch_attn"], wts["w_branch_pool"], wts["w_out"])
    h2f = h2.reshape(bsz * s, dm)
    eid, gate = _route_t(h2f, wts["peer_wq"], wts["peer_subkeys"])
    per_token = g2.shape[1] != 1
    g2f = g2.reshape(-1, 1, dm)
    out = _peer_t(eid, h2f, gate, x1.reshape(bsz * s, dm), g2f, wts["peer_uv"], 1 if per_token else s)
    return out.reshape(bsz, s, dm)


def kernel(x_prompt, x_sample, cache_k, cache_v, state_pool, c_prompt, c_sample, norm1_g, norm2_g, w_ada, b_ada, w_in, q_norm_g, k_norm_g, attn_sinks, w_pool, pool_scale, w_branch_attn, w_branch_pool, w_out, peer_wq, peer_subkeys, peer_u, peer_v):
    depth = w_ada.shape[0]
    bsz, seq, dm = x_prompt.shape
    nd = x_sample.shape[0]
    assert x_sample.shape[1] == 1
    yp = x_prompt
    ys = x_sample.reshape(1, nd, dm)
    kp, vp, pp, ksm, vsm, psm = [], [], [], [], [], []
    for l in range(depth):
        wts = {
            "norm2_g": norm2_g[l], "pool_scale": pool_scale[l],
            "w_pool": w_pool[l].astype(BF16), "w_branch_attn": w_branch_attn[l].astype(BF16),
            "w_branch_pool": w_branch_pool[l].astype(BF16), "w_out": w_out[l].astype(BF16),
            "peer_wq": peer_wq[l].astype(BF16),
            "peer_subkeys": peer_subkeys[l].reshape(2 * PEER_HEADS, PEER_NKEYS, -1).astype(BF16),
            "peer_uv": _pack_bf16_pair(peer_u[l], peer_v[l]).reshape(-1, LANES),
        }
        w_in_bf = w_in[l].astype(BF16)
        m = _ada(jnp.concatenate([c_prompt, c_sample], axis=0), w_ada[l].astype(BF16), b_ada[l])
        m6 = [m[:, i * dm:(i + 1) * dm] for i in range(6)]
        mp = [a[:bsz].reshape(bsz, 1, dm) for a in m6]
        ms = [a[bsz:].reshape(1, nd, dm) for a in m6]

        q, k, v, p, ga, gb = _in_proj(yp, mp[1], mp[0], norm1_g[l], w_in_bf)
        ao, kn = _attn_prompt(q, k, v, q_norm_g[l], k_norm_g[l], attn_sinks[l])
        kp.append(kn[:, -WINDOW:].reshape(bsz, WINDOW, N_KV_HEADS, HEAD_DIM))
        vp.append(v[:, -WINDOW:].reshape(bsz, WINDOW, N_KV_HEADS, HEAD_DIM))
        pp.append(p[:, -POOL_HIST:])
        yp = _channel_mix(yp, ao, _pool_prompt(p), ga, gb, (mp[2], mp[3], mp[4], mp[5]), wts)

        q, k, v, p, ga, gb = _in_proj(ys, ms[1], ms[0], norm1_g[l], w_in_bf)
        ck = cache_k[l]
        cv = cache_v[l]
        win = ck.shape[1]
        ao, kn = _attn_sample(q.reshape(nd, N_HEADS, HEAD_DIM), k.reshape(nd, N_KV_HEADS, HEAD_DIM),
                              v.reshape(nd, N_KV_HEADS, HEAD_DIM), ck.reshape(nd, win, KV_W),
                              cv.reshape(nd, win, KV_W), q_norm_g[l], k_norm_g[l], attn_sinks[l])
        ksm.append(jnp.concatenate([ck, kn[:, None]], axis=1)[:, -win:])
        vsm.append(jnp.concatenate([cv, v.reshape(nd, 1, N_KV_HEADS, HEAD_DIM)], axis=1)[:, -win:])
        st = state_pool[l]
        psm.append(jnp.concatenate([st, p.reshape(nd, 1, POOL_W)], axis=1)[:, -st.shape[1]:])
        d = _pool_sample(p.reshape(nd, POOL_W), st).reshape(1, nd, POOL_W)
        ys = _channel_mix(ys, ao.reshape(1, nd, ATT_W), d, ga, gb, (ms[2], ms[3], ms[4], ms[5]), wts)
    return (yp, ys.reshape(nd, 1, dm), jnp.stack(kp), jnp.stack(vp), jnp.stack(pp),
            jnp.stack(ksm), jnp.stack(vsm), jnp.stack(psm))
```

```python
import functools

import jax
import jax.numpy as jnp
from jax import lax
from jax.experimental import pallas as pl
from jax.experimental.pallas import tpu as pltpu

F32 = jnp.float32
BF16 = jnp.bfloat16
NORM_EPS = 1e-6
N_HEADS = 8
N_KV_HEADS = 2
Q_GROUP = N_HEADS // N_KV_HEADS
HEAD_DIM = 64
WINDOW = 128
ATT_W = N_HEADS * HEAD_DIM
KV_W = N_KV_HEADS * HEAD_DIM
POOL_WINDOWS = (2, 4, 8, 16)
POOL_GC = 128
POOL_W = POOL_GC * len(POOL_WINDOWS)
POOL_HIST = max(POOL_WINDOWS) - 1
HIST_ROWS = 16
PEER_HEADS = 8
PEER_NKEYS = 128
PEER_TOPK = 16
PEER_SLOTS = PEER_HEADS * PEER_TOPK
PEER_TOKENS = 8
LANES = 128
VMEM_LIMIT_BYTES = 48 * 1024 * 1024
NEG_INF = float("-inf")
NT_DIMS = (((1,), (1,)), ((), ()))


def _params(n_axes):
    return pltpu.CompilerParams(dimension_semantics=("arbitrary",) * n_axes,
                                vmem_limit_bytes=VMEM_LIMIT_BYTES)


def _rms(x, g):
    return x * lax.rsqrt(jnp.mean(x * x, axis=-1, keepdims=True) + NORM_EPS) * g


def _bdot(a, b):
    return jnp.dot(a.astype(BF16), b.astype(BF16), preferred_element_type=F32)


def _bdot_nt(a, b):
    return lax.dot_general(a.astype(BF16), b.astype(BF16), NT_DIMS, preferred_element_type=F32)


def _ada_kernel(c_ref, w_ref, b_ref, o_ref):
    c = c_ref[...]
    o_ref[...] = _bdot(c * jax.nn.sigmoid(c), w_ref[...]) + b_ref[...]


def _ada(c, w_bf, b):
    n, d = c.shape
    nout = w_bf.shape[1]
    tn = nout // 4
    return pl.pallas_call(
        _ada_kernel, grid=(nout // tn,),
        in_specs=[pl.BlockSpec((n, d), lambda j: (0, 0)),
                  pl.BlockSpec((d, tn), lambda j: (0, j)),
                  pl.BlockSpec((1, tn), lambda j: (0, j))],
        out_specs=pl.BlockSpec((n, tn), lambda j: (0, j)),
        out_shape=jax.ShapeDtypeStruct((n, nout), F32),
        compiler_params=_params(1), name="ada")(c, w_bf, b.reshape(1, nout))


_IN_CUTS = (0, ATT_W, ATT_W + KV_W, ATT_W + 2 * KV_W, ATT_W + 2 * KV_W + POOL_W)


def _in_kernel(x_ref, sc_ref, sh_ref, g_ref, w_ref, q_ref, k_ref, v_ref, p_ref, ga_ref, gb_ref):
    d = x_ref.shape[-1]
    h = _rms(x_ref[0], g_ref[...]) * (1.0 + sc_ref[0]) + sh_ref[0]
    hb = h.astype(BF16)
    cuts = _IN_CUTS + (_IN_CUTS[-1] + d, _IN_CUTS[-1] + 2 * d)
    for o_ref, lo, hi in zip((q_ref, k_ref, v_ref, p_ref, ga_ref, gb_ref), cuts[:-1], cuts[1:]):
        o_ref[0] = jnp.dot(hb, w_ref[:, lo:hi], preferred_element_type=F32)


def _mod_spec(mod, tm):
    if mod.shape[1] == 1:
        return pl.BlockSpec((1, 1, mod.shape[2]), lambda b, j: (b, 0, 0))
    return pl.BlockSpec((1, tm, mod.shape[2]), lambda b, j: (b, j, 0))


def _tile(s):
    return min(256, s)


def _in_proj(x, sc, sh, g, w_bf):
    bsz, s, d = x.shape
    tm = _tile(s)
    widths = (ATT_W, KV_W, KV_W, POOL_W, d, d)
    tok = lambda w: pl.BlockSpec((1, tm, w), lambda b, j: (b, j, 0))
    return pl.pallas_call(
        _in_kernel, grid=(bsz, s // tm),
        in_specs=[tok(d), _mod_spec(sc, tm), _mod_spec(sh, tm),
                  pl.BlockSpec((1, d), lambda b, j: (0, 0)),
                  pl.BlockSpec(w_bf.shape, lambda b, j: (0, 0))],
        out_specs=[tok(w) for w in widths],
        out_shape=[jax.ShapeDtypeStruct((bsz, s, w), F32) for w in widths],
        compiler_params=_params(2), name="in_proj")(x, sc, sh, g.reshape(1, d), w_bf)


def _attn_p_kernel(sinks_ref, q_ref, kc_ref, kp_ref, vc_ref, vp_ref, qg_ref, kg_ref, o_ref, kn_ref):
    j = pl.program_id(1)
    blk = q_ref.shape[1]
    qi = lax.broadcasted_iota(jnp.int32, (blk, 2 * blk), 0)
    kj = lax.broadcasted_iota(jnp.int32, (blk, 2 * blk), 1)
    dist = qi + blk - kj
    valid = (dist >= 0) & (dist <= WINDOW) & ((kj >= blk) | (j > 0))
    dist_f = dist.astype(F32)
    qg = qg_ref[...]
    kg = kg_ref[...]
    for kv in range(N_KV_HEADS):
        ks = slice(kv * HEAD_DIM, (kv + 1) * HEAD_DIM)
        kcn = _rms(kc_ref[0, :, ks], kg)
        kn_ref[0, :, ks] = kcn
        kk = jnp.concatenate([_rms(kp_ref[0, :, ks], kg), kcn], axis=0).astype(BF16)
        vv = jnp.concatenate([vp_ref[0, :, ks], vc_ref[0, :, ks]], axis=0).astype(BF16)
        for g in range(Q_GROUP):
            h = kv * Q_GROUP + g
            hs = slice(h * HEAD_DIM, (h + 1) * HEAD_DIM)
            qn = _rms(q_ref[0, :, hs], qg)
            s = _bdot_nt(qn, kk) * (HEAD_DIM ** -0.5)
            s = s - (2.0 ** -(h + 1)) * dist_f
            s = jnp.where(valid, s, -1e30)
            sink = sinks_ref[h]
            m = jnp.maximum(jnp.max(s, axis=-1, keepdims=True), sink)
            e = jnp.exp(s - m)
            den = jnp.sum(e, axis=-1, keepdims=True) + jnp.exp(sink - m)
            o_ref[0, :, hs] = jnp.dot((e / den).astype(BF16), vv, preferred_element_type=F32)


def _attn_prompt(q, k, v, qg, kg, sinks):
    bsz, s, _ = q.shape
    blk = WINDOW
    cur = lambda w: pl.BlockSpec((1, blk, w), lambda b, j: (b, j, 0))
    prev = lambda w: pl.BlockSpec((1, blk, w), lambda b, j: (b, jnp.maximum(j - 1, 0), 0))
    vec = pl.BlockSpec((1, HEAD_DIM), lambda b, j: (0, 0))
    return pl.pallas_call(
        _attn_p_kernel, grid=(bsz, s // blk),
        in_specs=[pl.BlockSpec(memory_space=pltpu.SMEM), cur(ATT_W), cur(KV_W), prev(KV_W),
                  cur(KV_W), prev(KV_W), vec, vec],
        out_specs=[cur(ATT_W), cur(KV_W)],
        out_shape=[jax.ShapeDtypeStruct((bsz, s, ATT_W), F32),
                   jax.ShapeDtypeStruct((bsz, s, KV_W), F32)],
        compiler_params=_params(2), name="attn_prompt")(
            sinks, q, k, k, v, v, qg.reshape(1, HEAD_DIM), kg.reshape(1, HEAD_DIM))


def _row_consts(vals):
    row = lax.broadcasted_iota(jnp.int32, (N_HEADS, 1), 0)
    out = jnp.zeros((N_HEADS, 1), F32)
    for h, v in enumerate(vals):
        out = jnp.where(row == h, v, out)
    return out


def _attn_s_kernel(sinks_ref, q_ref, kx_ref, vx_ref, ck_ref, cv_ref, qg_ref, kg_ref, o_ref, kn_ref):
    nb = q_ref.shape[0]
    w = ck_ref.shape[1]
    slope = _row_consts([2.0 ** -(h + 1) for h in range(N_HEADS)])
    sink = _row_consts([sinks_ref[h] for h in range(N_HEADS)])
    row = lax.broadcasted_iota(jnp.int32, (N_HEADS, 1), 0)
    first = row < Q_GROUP
    dist_c = (w - lax.broadcasted_iota(jnp.int32, (1, w), 1)).astype(F32)
    scale = HEAD_DIM ** -0.5
    for b in range(nb):
        qn = _rms(q_ref[b], qg_ref[...])
        kxn = _rms(kx_ref[b], kg_ref[...])
        kn_ref[b] = kxn
        vx = vx_ref[b]
        ck = ck_ref[b]
        cv = cv_ref[b]
        kx_h = jnp.where(first, kxn[0:1], kxn[1:2])
        vx_h = jnp.where(first, vx[0:1], vx[1:2])
        s_c = jnp.where(first, _bdot_nt(qn, ck[:, :HEAD_DIM]), _bdot_nt(qn, ck[:, HEAD_DIM:]))
        s_c = s_c * scale - slope * dist_c
        s_n = jnp.sum(qn * kx_h, axis=-1, keepdims=True) * scale
        m = jnp.maximum(jnp.maximum(jnp.max(s_c, axis=-1, keepdims=True), s_n), sink)
        e_c = jnp.exp(s_c - m)
        e_n = jnp.exp(s_n - m)
        den = jnp.sum(e_c, axis=-1, keepdims=True) + e_n + jnp.exp(sink - m)
        p_c = e_c / den
        o_c = jnp.where(first, _bdot(p_c, cv[:, :HEAD_DIM]), _bdot(p_c, cv[:, HEAD_DIM:]))
        o_ref[b] = o_c + (e_n / den) * vx_h


def _attn_sample(q, kx, vx, ck, cv, qg, kg, sinks):
    n = q.shape[0]
    w = ck.shape[1]
    nb = 8
    blk = lambda a, c: pl.BlockSpec((nb, a, c), lambda i: (i, 0, 0))
    vec = pl.BlockSpec((1, HEAD_DIM), lambda i: (0, 0))
    return pl.pallas_call(
        _attn_s_kernel, grid=(n // nb,),
        in_specs=[pl.BlockSpec(memory_space=pltpu.SMEM), blk(N_HEADS, HEAD_DIM),
                  blk(N_KV_HEADS, HEAD_DIM), blk(N_KV_HEADS, HEAD_DIM), blk(w, KV_W), blk(w, KV_W),
                  vec, vec],
        out_specs=[blk(N_HEADS, HEAD_DIM), blk(N_KV_HEADS, HEAD_DIM)],
        out_shape=[jax.ShapeDtypeStruct((n, N_HEADS, HEAD_DIM), F32),
                   jax.ShapeDtypeStruct((n, N_KV_HEADS, HEAD_DIM), F32)],
        compiler_params=_params(1), name="attn_sample")(
            sinks, q, kx, vx, ck, cv, qg.reshape(1, HEAD_DIM), kg.reshape(1, HEAD_DIM))


def _pool_p_kernel(p_ref, hist_ref, d_ref, ext_ref):
    j = pl.program_id(1)
    tm = p_ref.shape[1]
    cur = p_ref[0]
    ext_ref[0:HIST_ROWS, :] = hist_ref[0, 0]
    ext_ref[HIST_ROWS:, :] = cur
    pos = j * tm + lax.broadcasted_iota(jnp.int32, (tm, 1), 0)
    acc = cur
    lo = 1
    for gi, w in enumerate(POOL_WINDOWS):
        for s in range(lo, w):
            acc = acc + ext_ref[HIST_ROWS - s:HIST_ROWS - s + tm, :]
        lo = w
        gs = slice(gi * POOL_GC, (gi + 1) * POOL_GC)
        cnt = jnp.minimum(pos + 1, w).astype(F32)
        d_ref[0, :, gs] = acc[:, gs] / cnt - cur[:, gs]


def _pool_prompt(p):
    bsz, s, pw = p.shape
    tm = _tile(s)
    nt = s // tm
    tails = p.reshape(bsz, nt, tm, pw)[:, :, tm - HIST_ROWS:, :]
    hist = jnp.concatenate([jnp.zeros((bsz, 1, HIST_ROWS, pw), F32), tails[:, :-1]], axis=1)
    return pl.pallas_call(
        _pool_p_kernel, grid=(bsz, nt),
        in_specs=[pl.BlockSpec((1, tm, pw), lambda b, j: (b, j, 0)),
                  pl.BlockSpec((1, 1, HIST_ROWS, pw), lambda b, j: (b, j, 0, 0))],
        out_specs=pl.BlockSpec((1, tm, pw), lambda b, j: (b, j, 0)),
        out_shape=jax.ShapeDtypeStruct((bsz, s, pw), F32),
        scratch_shapes=[pltpu.VMEM((tm + HIST_ROWS, pw), F32)],
        compiler_params=_params(2), name="pool_prompt")(p, hist)


def _pool_s_kernel(p_ref, st_ref, d_ref):
    n_hist = st_ref.shape[1]
    cur = p_ref[...]
    acc = cur
    lo = 1
    for gi, w in enumerate(POOL_WINDOWS):
        for s in range(lo, w):
            acc = acc + st_ref[:, n_hist - s, :]
        lo = w
        gs = slice(gi * POOL_GC, (gi + 1) * POOL_GC)
        d_ref[:, gs] = acc[:, gs] / float(w) - cur[:, gs]


def _pool_sample(p, state):
    assert state.shape[1] == POOL_HIST
    return pl.pallas_call(
        _pool_s_kernel, out_shape=jax.ShapeDtypeStruct(p.shape, F32),
        compiler_params=pltpu.CompilerParams(vmem_limit_bytes=VMEM_LIMIT_BYTES),
        name="pool_sample")(p, state)


def _mix_kernel(x_ref, ao_ref, d_ref, ga_ref, gb_ref, g1_ref, sc2_ref, sh2_ref, n2_ref,
                wp_ref, ps_ref, wba_ref, wbp_ref, wo_ref, x1_ref, h2_ref):
    d = d_ref[0]
    po = jnp.concatenate(
        [_bdot(d[:, gi * POOL_GC:(gi + 1) * POOL_GC], wp_ref[gi]) for gi in range(len(POOL_WINDOWS))],
        axis=-1) * ps_ref[...]
    merged = (jax.nn.sigmoid(ga_ref[0]) * _bdot(ao_ref[0], wba_ref[...])
              + jax.nn.sigmoid(gb_ref[0]) * _bdot(po, wbp_ref[...]))
    x1 = x_ref[0] + g1_ref[0] * _bdot(merged, wo_ref[...])
    x1_ref[0] = x1
    h2_ref[0] = _rms(x1, n2_ref[...]) * (1.0 + sc2_ref[0]) + sh2_ref[0]


def _mix(x, ao, d, ga, gb, g1, sc2, sh2, n2g, wp_bf, pscale, wba_bf, wbp_bf, wo_bf):
    bsz, s, dm = x.shape
    tm = _tile(s)
    tok = lambda w: pl.BlockSpec((1, tm, w), lambda b, j: (b, j, 0))
    full = lambda a: pl.BlockSpec(a.shape, lambda b, j: (0,) * a.ndim)
    n2g = n2g.reshape(1, dm)
    pscale = pscale.reshape(1, POOL_W)
    return pl.pallas_call(
        _mix_kernel, grid=(bsz, s // tm),
        in_specs=[tok(dm), tok(ATT_W), tok(POOL_W), tok(dm), tok(dm),
                  _mod_spec(g1, tm), _mod_spec(sc2, tm), _mod_spec(sh2, tm), full(n2g),
                  full(wp_bf), full(pscale), full(wba_bf), full(wbp_bf), full(wo_bf)],
        out_specs=[tok(dm), tok(dm)],
        out_shape=[jax.ShapeDtypeStruct((bsz, s, dm), F32)] * 2,
        compiler_params=_params(2), name="mix")(
            x, ao, d, ga, gb, g1, sc2, sh2, n2g, wp_bf, pscale, wba_bf, wbp_bf, wo_bf)


def _route_kernel(h2_ref, wq_ref, sk_ref, eid_ref, gate_ref, sc_ref):
    tm = h2_ref.shape[0]
    hb = h2_ref[...].astype(BF16)
    for hc in range(2 * PEER_HEADS):
        q = jnp.dot(hb, wq_ref[:, hc * LANES:(hc + 1) * LANES], preferred_element_type=F32)
        sc_ref[hc] = _bdot_nt(q, sk_ref[hc])
    lane = lax.broadcasted_iota(jnp.int32, (tm, LANES), 1)
    lane_f = lane.astype(F32)
    lane_hi = lax.shift_right_logical(lane, 4)
    lane_lo = lane & (PEER_TOPK - 1)
    zeros = jnp.zeros((tm, LANES), F32)
    half = LANES // PEER_TOPK

    def take_max(s):
        m = jnp.max(s, axis=-1, keepdims=True)
        idx = jnp.min(jnp.where(s == m, lane_f, float(LANES)), axis=-1, keepdims=True)
        return m, idx, jnp.where(lane_f == idx, NEG_INF, s)

    def head_body(h, _):
        def first_half(i, c):
            s, v_lo, v_hi, i_lo, i_hi = c
            m, idx, s = take_max(s)
            at_lo = lane_hi == i
            at_hi = lane_hi == i - half
            return (s, jnp.where(at_lo, m, v_lo), jnp.where(at_hi, m, v_hi),
                    jnp.where(at_lo, idx, i_lo), jnp.where(at_hi, idx, i_hi))

        def second_half(j, c):
            s, v, ix = c
            m, idx, s = take_max(s)
            at = lane_lo == j
            return s, jnp.where(at, m, v), jnp.where(at, idx, ix)

        _, a_lo, a_hi, ia_lo, ia_hi = lax.fori_loop(
            0, PEER_TOPK, first_half, (sc_ref[2 * h], zeros, zeros, zeros, zeros))
        _, b_t, ib_t = lax.fori_loop(0, PEER_TOPK, second_half, (sc_ref[2 * h + 1], zeros, zeros))
        e_lo = ia_lo * float(PEER_NKEYS) + ib_t
        e_hi = ia_hi * float(PEER_NKEYS) + ib_t

        def pick(k, c):
            c_lo, c_hi, best, eids = c
            m = jnp.maximum(jnp.max(c_lo, axis=-1, keepdims=True), jnp.max(c_hi, axis=-1, keepdims=True))
            p_lo = jnp.min(jnp.where(c_lo == m, lane_f, 2.0 * LANES), axis=-1, keepdims=True)
            p_hi = jnp.min(jnp.where(c_hi == m, lane_f + LANES, 2.0 * LANES), axis=-1, keepdims=True)
            p = jnp.minimum(p_lo, p_hi)
            sel_lo = lane_f == p
            sel_hi = lane_f + LANES == p
            e = jnp.maximum(jnp.max(jnp.where(sel_lo, e_lo, -1.0), axis=-1, keepdims=True),
                            jnp.max(jnp.where(sel_hi, e_hi, -1.0), axis=-1, keepdims=True))
            at = lane == k
            return (jnp.where(sel_lo, NEG_INF, c_lo), jnp.where(sel_hi, NEG_INF, c_hi),
                    jnp.where(at, m, best), jnp.where(at, e, eids))

        _, _, best, eids = lax.fori_loop(0, PEER_TOPK, pick, (a_lo + b_t, a_hi + b_t, zeros, zeros))
        live = lane < PEER_TOPK
        mx = jnp.max(jnp.where(live, best, NEG_INF), axis=-1, keepdims=True)
        ex = jnp.where(live, jnp.exp(best - mx), 0.0)
        gate_ref[h] = ex / jnp.sum(ex, axis=-1, keepdims=True)
        eid_ref[h] = eids.astype(jnp.int32)
        return 0

    lax.fori_loop(0, PEER_HEADS, head_body, 0)


def _route(h2, wq_bf, sk_bf):
    t, dm = h2.shape
    tm = min(128, t)
    out = pl.BlockSpec((PEER_HEADS, tm, LANES), lambda i: (0, i, 0))
    eid, gate = pl.pallas_call(
        _route_kernel, grid=(t // tm,),
        in_specs=[pl.BlockSpec((tm, dm), lambda i: (i, 0)),
                  pl.BlockSpec(wq_bf.shape, lambda i: (0, 0)),
                  pl.BlockSpec(sk_bf.shape, lambda i: (0, 0, 0))],
        out_specs=[out, out],
        out_shape=[jax.ShapeDtypeStruct((PEER_HEADS, t, LANES), jnp.int32),
                   jax.ShapeDtypeStruct((PEER_HEADS, t, LANES), F32)],
        scratch_shapes=[pltpu.VMEM((2 * PEER_HEADS, tm, LANES), F32)],
        compiler_params=_params(1), name="route")(h2, wq_bf, sk_bf)
    flat = lambda a: jnp.transpose(a[:, :, :PEER_TOPK], (1, 0, 2)).reshape(t, PEER_SLOTS)
    return flat(eid), flat(gate)


def _gelu_tanh(x):
    return x * (0.5 * (1.0 + jnp.tanh(0.7978845608028654 * (x + 0.044715 * (x * x * x)))))


def _peer_kernel(eid_ref, h2_ref, gate_ref, x1_ref, g2_ref, u_hbm, v_hbm, o_ref, ubuf, vbuf, sem):
    n_tok = h2_ref.shape[0]

    def row_copies(t, k):
        e = eid_ref[t, k]
        r = t * PEER_SLOTS + k
        return (pltpu.make_async_copy(u_hbm.at[pl.ds(e, 1)], ubuf.at[pl.ds(r, 1)], sem.at[0]),
                pltpu.make_async_copy(v_hbm.at[pl.ds(e, 1)], vbuf.at[pl.ds(r, 1)], sem.at[1]))

    def start_row(r, _):
        cu, cv = row_copies(lax.shift_right_logical(r, 7), r & (PEER_SLOTS - 1))
        cu.start()
        cv.start()
        return 0

    def wait_row(r, _):
        cu, cv = row_copies(lax.shift_right_logical(r, 7), r & (PEER_SLOTS - 1))
        cu.wait()
        cv.wait()
        return 0

    lax.fori_loop(0, n_tok * PEER_SLOTS, start_row, 0)
    lax.fori_loop(0, n_tok * PEER_SLOTS, wait_row, 0)

    hb = h2_ref[...].astype(BF16)
    gate = gate_ref[...]
    row = lax.broadcasted_iota(jnp.int32, (n_tok, 1), 0)
    acc = jnp.zeros(o_ref.shape, F32)
    for t in range(n_tok):
        rows = pl.ds(t * PEER_SLOTS, PEER_SLOTS)
        act = _gelu_tanh(_bdot_nt(hb, ubuf[rows, :]))
        out_t = _bdot(gate * act, vbuf[rows, :])
        acc = jnp.where(row == t, out_t, acc)
    o_ref[...] = x1_ref[...] + g2_ref[:, 0, :] * acc


def _peer(eid, h2, gate, x1, g2, u_tab, v_tab, rows_per_mod):
    t, dm = h2.shape
    nt = PEER_TOKENS
    if rows_per_mod == 1:
        g2_spec = pl.BlockSpec((nt, 1, dm), lambda i: (i, 0, 0))
    else:
        assert rows_per_mod % nt == 0
        g2_spec = pl.BlockSpec((1, 1, dm), lambda i: ((i * nt) // rows_per_mod, 0, 0))
    tok = lambda w: pl.BlockSpec((nt, w), lambda i: (i, 0))
    return pl.pallas_call(
        _peer_kernel, grid=(t // nt,),
        in_specs=[pl.BlockSpec((nt, PEER_SLOTS), lambda i: (i, 0), memory_space=pltpu.SMEM),
                  tok(dm), tok(PEER_SLOTS), tok(dm), g2_spec,
                  pl.BlockSpec(memory_space=pl.ANY), pl.BlockSpec(memory_space=pl.ANY)],
        out_specs=tok(dm),
        out_shape=jax.ShapeDtypeStruct((t, dm), F32),
        scratch_shapes=[pltpu.VMEM((nt * PEER_SLOTS, dm), F32),
                        pltpu.VMEM((nt * PEER_SLOTS, dm), F32),
                        pltpu.SemaphoreType.DMA((2,))],
        compiler_params=_params(1), name="peer")(eid, h2, gate, x1, g2, u_tab, v_tab)


def _route_t_kernel(h2_ref, wq_ref, sk_ref, eid_ref, gate_ref, sc_ref):
    tm = h2_ref.shape[0]
    hb = h2_ref[...].astype(BF16)
    for hc in range(2 * PEER_HEADS):
        q = jnp.dot(hb, wq_ref[:, hc * LANES:(hc + 1) * LANES], preferred_element_type=F32)
        sc_ref[hc] = _bdot_nt(sk_ref[hc], q)
    key_f = lax.broadcasted_iota(jnp.int32, (PEER_NKEYS, tm), 0).astype(F32)
    n_cand = PEER_TOPK * PEER_TOPK
    cand_f = lax.broadcasted_iota(jnp.int32, (n_cand, tm), 0).astype(F32)
    rank = lax.broadcasted_iota(jnp.int32, (PEER_TOPK, tm), 0)
    zeros = jnp.zeros((PEER_TOPK, tm), F32)

    def top_half(s):
        def body(i, c):
            s, v, ix = c
            m = jnp.max(s, axis=0, keepdims=True)
            idx = jnp.min(jnp.where(s == m, key_f, float(PEER_NKEYS)), axis=0, keepdims=True)
            at = rank == i
            return jnp.where(key_f == idx, NEG_INF, s), jnp.where(at, m, v), jnp.where(at, idx, ix)
        _, v, ix = lax.fori_loop(0, PEER_TOPK, body, (s, zeros, zeros))
        return v, ix

    def head_body(h, _):
        a, ia = top_half(sc_ref[2 * h])
        b, ib = top_half(sc_ref[2 * h + 1])
        cand = jnp.concatenate([a[i:i + 1] + b for i in range(PEER_TOPK)], axis=0)
        expert = jnp.concatenate([ia[i:i + 1] * float(PEER_NKEYS) + ib for i in range(PEER_TOPK)], axis=0)

        def pick(k, c):
            cd, best, eids = c
            m = jnp.max(cd, axis=0, keepdims=True)
            p = jnp.min(jnp.where(cd == m, cand_f, float(n_cand)), axis=0, keepdims=True)
            sel = cand_f == p
            e = jnp.max(jnp.where(sel, expert, -1.0), axis=0, keepdims=True)
            at = rank == k
            return jnp.where(sel, NEG_INF, cd), jnp.where(at, m, best), jnp.where(at, e, eids)

        _, best, eids = lax.fori_loop(0, PEER_TOPK, pick, (cand, zeros, zeros))
        ex = jnp.exp(best - jnp.max(best, axis=0, keepdims=True))
        gate_ref[h] = ex / jnp.sum(ex, axis=0, keepdims=True)
        eid_ref[h] = eids.astype(jnp.int32)
        return 0

    lax.fori_loop(0, PEER_HEADS, head_body, 0)


def _route_t(h2, wq_bf, sk_bf):
    t, dm = h2.shape
    tm = min(LANES, t)
    out = pl.BlockSpec((PEER_HEADS, PEER_TOPK, tm), lambda i: (0, 0, i))
    eid, gate = pl.pallas_call(
        _route_t_kernel, grid=(t // tm,),
        in_specs=[pl.BlockSpec((tm, dm), lambda i: (i, 0)),
                  pl.BlockSpec(wq_bf.shape, lambda i: (0, 0)),
                  pl.BlockSpec(sk_bf.shape, lambda i: (0, 0, 0))],
        out_specs=[out, out],
        out_shape=[jax.ShapeDtypeStruct((PEER_HEADS, PEER_TOPK, t), jnp.int32),
                   jax.ShapeDtypeStruct((PEER_HEADS, PEER_TOPK, t), F32)],
        scratch_shapes=[pltpu.VMEM((2 * PEER_HEADS, PEER_NKEYS, tm), F32)],
        compiler_params=_params(1), name="route")(h2, wq_bf, sk_bf)
    return eid.reshape(PEER_SLOTS, t).T, gate.reshape(PEER_SLOTS, t).T


PEER_STEP_TOKENS = 16
PEER_GROUP = 8


def _peer_db_kernel(eid0_ref, eidn_ref, h2_ref, gate_ref, x1_ref, g2_ref, uv_hbm, o_ref, buf, sem):
    i = pl.program_id(0)
    n_tok, dm = h2_ref.shape
    n_rows = n_tok * PEER_SLOTS
    slot = i & 1

    def issue(eid_ref, dst_slot):
        def body(r2, _):
            for par in range(2):
                r = 2 * r2 + par
                e = eid_ref[lax.shift_right_logical(r, 7), r & (PEER_SLOTS - 1)]
                pltpu.make_async_copy(uv_hbm.at[pl.ds(e, 1)], buf.at[dst_slot, pl.ds(r, 1)],
                                      sem.at[dst_slot]).start(priority=par)
            return 0
        lax.fori_loop(0, n_rows // 2, body, 0, unroll=4)

    @pl.when(i == 0)
    def _():
        issue(eid0_ref, 0)

    @pl.when(i + 1 < pl.num_programs(0))
    def _():
        issue(eidn_ref, 1 - slot)

    pltpu.make_async_copy(uv_hbm.at[pl.ds(0, n_rows)], buf.at[slot], sem.at[slot]).wait()

    row = lax.broadcasted_iota(jnp.int32, (PEER_GROUP, 1), 0)
    g2 = g2_ref[:, 0, :]
    for g in range(n_tok // PEER_GROUP):
        gs = slice(g * PEER_GROUP, (g + 1) * PEER_GROUP)
        hb = h2_ref[gs, :].astype(BF16)
        gate = gate_ref[gs, :]
        acc = jnp.zeros((PEER_GROUP, dm), F32)
        for tt in range(PEER_GROUP):
            rows = pl.ds((g * PEER_GROUP + tt) * PEER_SLOTS, PEER_SLOTS)
            w = buf[slot, rows, :]
            u_t = lax.bitcast_convert_type(w << 16, F32).astype(BF16)
            v_t = lax.bitcast_convert_type(w & jnp.uint32(0xFFFF0000), F32).astype(BF16)
            act = _gelu_tanh(lax.dot_general(hb, u_t, NT_DIMS, preferred_element_type=F32))
            out_t = jnp.dot((gate * act).astype(BF16), v_t, preferred_element_type=F32)
            acc = jnp.where(row == tt, out_t, acc)
        o_ref[gs, :] = x1_ref[gs, :] + (g2 if g2.shape[0] == 1 else g2[gs, :]) * acc


def _peer_db(eid, h2, gate, x1, g2, uv_bf, rows_per_mod):
    t, dm = h2.shape
    nt = PEER_STEP_TOKENS
    n_steps = t // nt
    assert t % nt == 0 and uv_bf.shape[1] == dm and uv_bf.dtype == jnp.uint32
    if rows_per_mod == 1:
        g2_spec = pl.BlockSpec((nt, 1, dm), lambda i: (i, 0, 0))
    else:
        assert rows_per_mod % nt == 0
        g2_spec = pl.BlockSpec((1, 1, dm), lambda i: ((i * nt) // rows_per_mod, 0, 0))
    tok = lambda w: pl.BlockSpec((nt, w), lambda i: (i, 0))
    return pl.pallas_call(
        _peer_db_kernel, grid=(n_steps,),
        in_specs=[pl.BlockSpec((nt, PEER_SLOTS), lambda i: (i, 0), memory_space=pltpu.SMEM),
                  pl.BlockSpec((nt, PEER_SLOTS), lambda i: (jnp.minimum(i + 1, n_steps - 1), 0),
                               memory_space=pltpu.SMEM),
                  tok(dm), tok(PEER_SLOTS), tok(dm), g2_spec, pl.BlockSpec(memory_space=pl.ANY)],
        out_specs=tok(dm),
        out_shape=jax.ShapeDtypeStruct((t, dm), F32),
        scratch_shapes=[pltpu.VMEM((2, nt * PEER_SLOTS, dm), jnp.uint32),
                        pltpu.SemaphoreType.DMA((2,))],
        compiler_params=_params(1), name="peer")(eid, eid, h2, gate, x1, g2, uv_bf)


SUBLANES = 8
ROW_TILES = 8


def _dot_f32_rhs01(a, b01):
    hi = a.astype(BF16)
    r1 = a - hi.astype(F32)
    mid = r1.astype(BF16)
    lo = (r1 - mid.astype(F32)).astype(BF16)
    d = lambda x: jnp.dot(x, b01, preferred_element_type=F32)
    return d(hi) + d(mid) + d(lo)


def _peer_t_kernel(eid0_ref, eidn_ref, h8_ref, gate_ref, x8_ref, g8_ref, et_ref, e_ref, uv_hbm, o_ref, buf, sem):
    i = pl.program_id(0)
    n_tok = h8_ref.shape[0]
    n_rows = n_tok * PEER_SLOTS
    slot = i & 1
    dm = ROW_TILES * LANES

    def issue(eid_ref, dst_slot):
        def body(r2, _):
            for par in range(2):
                r = 2 * r2 + par
                e = eid_ref[0, 0, r]
                pltpu.make_async_copy(uv_hbm.at[e], buf.at[dst_slot, r], sem.at[dst_slot]).start(priority=par)
            return 0
        lax.fori_loop(0, n_rows // 2, body, 0, unroll=16)

    @pl.when(i == 0)
    def _():
        issue(eid0_ref, 0)

    @pl.when(i + 1 < pl.num_programs(0))
    def _():
        issue(eidn_ref, 1 - slot)

    pltpu.make_async_copy(uv_hbm.at[pl.ds(0, n_rows)], buf.at[slot], sem.at[slot]).wait()

    col = lax.broadcasted_iota(jnp.int32, (SUBLANES, dm), 1)
    diag = (col & (ROW_TILES - 1)) == lax.broadcasted_iota(jnp.int32, (SUBLANES, dm), 0)
    row = lax.broadcasted_iota(jnp.int32, (SUBLANES, 1), 0)
    tok_rows = PEER_SLOTS * ROW_TILES
    for g in range(n_tok // SUBLANES):
        partial = jnp.zeros((SUBLANES, dm), F32)
        for tt in range(SUBLANES):
            t = g * SUBLANES + tt
            w = buf[slot, pl.ds(t * PEER_SLOTS, PEER_SLOTS)].reshape(tok_rows, LANES)
            u_t = lax.bitcast_convert_type(w << 16, F32).astype(BF16)
            p = lax.dot_general(h8_ref[t].astype(BF16), u_t, NT_DIMS, preferred_element_type=F32)
            r = jnp.sum(jnp.where(diag, p, 0.0), axis=0, keepdims=True)
            partial = jnp.where(row == tt, r, partial)
        act = _gelu_tanh(_dot_f32_rhs01(partial, et_ref[...]))
        wgt = (gate_ref[g * SUBLANES:(g + 1) * SUBLANES, :] * act).astype(BF16)
        w_rep = jnp.dot(wgt, e_ref[...], preferred_element_type=F32)
        for tt in range(SUBLANES):
            t = g * SUBLANES + tt
            w = buf[slot, pl.ds(t * PEER_SLOTS, PEER_SLOTS)].reshape(tok_rows, LANES)
            v_t = lax.bitcast_convert_type(w & jnp.uint32(0xFFFF0000), F32).astype(BF16)
            w8 = jnp.where(diag, w_rep[tt:tt + 1], 0.0).astype(BF16)
            out = jnp.dot(w8, v_t, preferred_element_type=F32)
            g8 = g8_ref[0] if g8_ref.shape[0] == 1 else g8_ref[t]
            o_ref[t] = x8_ref[t] + g8 * out


def _peer_t(eid, h2, gate, x1, g2, uv_tiles, rows_per_mod):
    t, dm = h2.shape
    nt = PEER_STEP_TOKENS
    n_steps = t // nt
    assert t % nt == 0 and dm == ROW_TILES * LANES and uv_tiles.shape[1] == LANES
    tile3 = lambda a: a.reshape(-1, ROW_TILES, LANES)
    eid = eid.reshape(n_steps, 1, nt * PEER_SLOTS)
    eid_spec = lambda f: pl.BlockSpec((1, 1, nt * PEER_SLOTS), lambda i: (f(i), 0, 0), memory_space=pltpu.SMEM)
    if rows_per_mod == 1:
        g_spec = pl.BlockSpec((nt, ROW_TILES, LANES), lambda i: (i, 0, 0))
    else:
        assert rows_per_mod % nt == 0
        g_spec = pl.BlockSpec((1, ROW_TILES, LANES), lambda i: ((i * nt) // rows_per_mod, 0, 0))
    tok3 = pl.BlockSpec((nt, ROW_TILES, LANES), lambda i: (i, 0, 0))
    col = jnp.arange(dm, dtype=jnp.int32)
    expand = (col[None, :] // ROW_TILES == jnp.arange(PEER_SLOTS, dtype=jnp.int32)[:, None]).astype(BF16)
    const = lambda a: pl.BlockSpec(a.shape, lambda i: (0, 0))
    out = pl.pallas_call(
        _peer_t_kernel, grid=(n_steps,),
        in_specs=[eid_spec(lambda i: i), eid_spec(lambda i: jnp.minimum(i + 1, n_steps - 1)),
                  tok3, pl.BlockSpec((nt, PEER_SLOTS), lambda i: (i, 0)), tok3, g_spec,
                  const(expand.T), const(expand), pl.BlockSpec(memory_space=pl.ANY)],
        out_specs=tok3,
        out_shape=jax.ShapeDtypeStruct((t, ROW_TILES, LANES), F32),
        scratch_shapes=[pltpu.VMEM((2, nt * PEER_SLOTS, ROW_TILES, LANES), jnp.uint32),
                        pltpu.SemaphoreType.DMA((2,))],
        compiler_params=_params(1), name="peer")(
            eid, eid, tile3(h2), gate, tile3(x1), tile3(g2), expand.T, expand, tile3(uv_tiles))
    return out.reshape(t, dm)


def _pack_bf16_pair(lo, hi):
    bits = lambda a: lax.bitcast_convert_type(a.astype(BF16), jnp.uint16).astype(jnp.uint32)
    return bits(lo) | (bits(hi) << 16)


def _channel_mix(x, ao, d, ga, gb, mods, wts):
    g1, sh2, sc2, g2 = mods
    bsz, s, dm = x.shape
    x1, h2 = _mix(x, ao, d, ga, gb, g1, sc2, sh2, wts["norm2_g"], wts["w_pool"], wts["pool_scale"],
                  wts["w_branch_attn"], wts["w_branch_pool"], wts["w_out"])
    h2f = h2.reshape(bsz * s, dm)
    eid, gate = _route_t(h2f, wts["peer_wq"], wts["peer_subkeys"])
    per_token = g2.shape[1] != 1
    g2f = g2.reshape(-1, 1, dm)
    out = _peer_t(eid, h2f, gate, x1.reshape(bsz * s, dm), g2f, wts["peer_uv"], 1 if per_token else s)
    return out.reshape(bsz, s, dm)


def kernel(x_prompt, x_sample, cache_k, cache_v, state_pool, c_prompt, c_sample, norm1_g, norm2_g, w_ada, b_ada, w_in, q_norm_g, k_norm_g, attn_sinks, w_pool, pool_scale, w_branch_attn, w_branch_pool, w_out, peer_wq, peer_subkeys, peer_u, peer_v):
    depth = w_ada.shape[0]
    bsz, seq, dm = x_prompt.shape
    nd = x_sample.shape[0]
    assert x_sample.shape[1] == 1
    yp = x_prompt
    ys = x_sample.reshape(1, nd, dm)
    kp, vp, pp, ksm, vsm, psm = [], [], [], [], [], []
    for l in range(depth):
        wts = {
            "norm2_g": norm2_g[l], "pool_scale": pool_scale[l],
            "w_pool": w_pool[l].astype(BF16), "w_branch_attn": w_branch_attn[l].astype(BF16),
            "w_branch_pool": w_branch_pool[l].astype(BF16), "w_out": w_out[l].astype(BF16),
            "peer_wq": peer_wq[l].astype(BF16),
            "peer_subkeys": peer_subkeys[l].reshape(2 * PEER_HEADS, PEER_NKEYS, -1).astype(BF16),
            "peer_uv": _pack_bf16_pair(peer_u[l], peer_v[l]).reshape(-1, LANES),
        }
        w_in_bf = w_in[l].astype(BF16)
        m = _ada(jnp.concatenate([c_prompt, c_sample], axis=0), w_ada[l].astype(BF16), b_ada[l])
        m6 = [m[:, i * dm:(i + 1) * dm] for i in range(6)]
        mp = [a[:bsz].reshape(bsz, 1, dm) for a in m6]
        ms = [a[bsz:].reshape(1, nd, dm) for a in m6]

        q, k, v, p, ga, gb = _in_proj(yp, mp[1], mp[0], norm1_g[l], w_in_bf)
        ao, kn = _attn_prompt(q, k, v, q_norm_g[l], k_norm_g[l], attn_sinks[l])
        kp.append(kn[:, -WINDOW:].reshape(bsz, WINDOW, N_KV_HEADS, HEAD_DIM))
        vp.append(v[:, -WINDOW:].reshape(bsz, WINDOW, N_KV_HEADS, HEAD_DIM))
        pp.append(p[:, -POOL_HIST:])
        yp = _channel_mix(yp, ao, _pool_prompt(p), ga, gb, (mp[2], mp[3], mp[4], mp[5]), wts)

        q, k, v, p, ga, gb = _in_proj(ys, ms[1], ms[0], norm1_g[l], w_in_bf)
        ck = cache_k[l]
        cv = cache_v[l]
        win = ck.shape[1]
        ao, kn = _attn_sample(q.reshape(nd, N_HEADS, HEAD_DIM), k.reshape(nd, N_KV_HEADS, HEAD_DIM),
                              v.reshape(nd, N_KV_HEADS, HEAD_DIM), ck.reshape(nd, win, KV_W),
                              cv.reshape(nd, win, KV_W), q_norm_g[l], k_norm_g[l], attn_sinks[l])
        ksm.append(jnp.concatenate([ck, kn[:, None]], axis=1)[:, -win:])
        vsm.append(jnp.concatenate([cv, v.reshape(nd, 1, N_KV_HEADS, HEAD_DIM)], axis=1)[:, -win:])
        st = state_pool[l]
        psm.append(jnp.concatenate([st, p.reshape(nd, 1, POOL_W)], axis=1)[:, -st.shape[1]:])
        d = _pool_sample(p.reshape(nd, POOL_W), st).reshape(1, nd, POOL_W)
        ys = _channel_mix(ys, ao.reshape(1, nd, ATT_W), d, ga, gb, (ms[2], ms[3], ms[4], ms[5]), wts)
    return (yp, ys.reshape(nd, 1, dm), jnp.stack(kp), jnp.stack(vp), jnp.stack(pp),
            jnp.stack(ksm), jnp.stack(vsm), jnp.stack(psm))
```
